```python
import math
import jax, jax.numpy as jnp
from jax import lax
import numpy as np

D_MODEL = 1024
BATCH = 4
SEQ = 8192
DEPTH = 4
DEC_BATCH = 8
DEC_SEQ = 16
PAST_LEN = 2048

CHUNK = 64
D_MIX = D_MODEL
BRANCH = D_MIX // 4
NORM_EPS = 1e-6
RW_HEADS = 4
RW_HD = BRANCH // RW_HEADS
RW_LORA_W = 64
RW_LORA_A = 64
RW_SHIFT = 3 * BRANCH + RW_LORA_W + RW_LORA_A
RW_GN_EPS = 64e-5
MLA_HEADS = 4
NOPE = 64
ROPE = 32
VHD = BRANCH // MLA_HEADS
QK_HD = NOPE + ROPE
Q_RANK = 192
KV_RANK = 128
ROPE_THETA = 10000.0
MLA_QBLOCK = 128
SGU_CHUNK = 128
SGU_HEADS = 4
SGU_HD = BRANCH // SGU_HEADS
SGU_EPS = 1e-5
POOL_GROUPS = 4
POOL_WINDOWS = (2, 4, 8, 16)
POOL_GD = BRANCH // POOL_GROUPS
POOL_HIST = max(POOL_WINDOWS) - 1
OFF_A = 0
OFF_B = OFF_A + RW_SHIFT
OFF_C = OFF_B + Q_RANK + KV_RANK + ROPE
OFF_D = OFF_C + 2 * BRANCH
OFF_G = OFF_D + BRANCH
D_IN = OFF_G + 4 * BRANCH

kernel_name = 'hymba_style_streaming_rwkv7_mla_sgu_pool'


def rms_norm(x, g, eps=NORM_EPS):
    xf = x.astype(jnp.float32)
    y = xf * lax.rsqrt(jnp.mean(xf * xf, axis=-1, keepdims=True) + eps)
    return (y * g.astype(jnp.float32)).astype(x.dtype)


def layer_norm(x, g, b, eps):
    xf = x.astype(jnp.float32)
    xc = xf - jnp.mean(xf, axis=-1, keepdims=True)
    var = jnp.mean(xc * xc, axis=-1, keepdims=True)
    return (xc * lax.rsqrt(var + eps) * g.astype(jnp.float32) + b.astype(jnp.float32)).astype(x.dtype)


def rope(x, pos):
    half = ROPE // 2
    inv = ROPE_THETA ** (-jnp.arange(half, dtype=jnp.float32) / half)
    ang = pos.astype(jnp.float32)[:, None] * inv[None, :]
    shape = (pos.shape[0],) + (1,) * (x.ndim - 3) + (half,)
    cos = jnp.cos(ang).reshape(shape)
    sin = jnp.sin(ang).reshape(shape)
    xf = x.astype(jnp.float32)
    x1, x2 = xf[..., :half], xf[..., half:]
    return jnp.concatenate([x1 * cos - x2 * sin, x1 * sin + x2 * cos], axis=-1).astype(x.dtype)


def rwkv7_mix(za, shift_prev, s0, p):
    f32 = jnp.float32
    b, t, _ = za.shape
    prev = jnp.concatenate([shift_prev[:, None, :].astype(za.dtype), za[:, :-1]], axis=1)
    zs = (za + p['rw_mu'] * (prev - za)).astype(f32)
    r = zs[..., :BRANCH]
    k = zs[..., BRANCH:2 * BRANCH]
    v = zs[..., 2 * BRANCH:3 * BRANCH]
    wd = zs[..., 3 * BRANCH:3 * BRANCH + RW_LORA_W]
    ad = zs[..., 3 * BRANCH + RW_LORA_W:]
    w_log = -jax.nn.softplus(-(p['rw_w0'].astype(f32) + jnp.tanh(wd) @ p['rw_w2'].astype(f32))) - 0.5
    decay = jnp.exp(-jnp.exp(w_log))
    a = jax.nn.sigmoid(p['rw_a0'].astype(f32) + ad @ p['rw_a2'].astype(f32))
    heads = lambda u: u.reshape(b, t, RW_HEADS, RW_HD)
    kk = heads(k * p['rw_kk'].astype(f32))
    kk = kk * lax.rsqrt(jnp.maximum(jnp.sum(kk * kk, axis=-1, keepdims=True), 1e-24))
    k = k * (1.0 + (a - 1.0) * p['rw_ka'].astype(f32))
    r, k, v, decay, a = heads(r), heads(k), heads(v), heads(decay), heads(a)

    def step(s, inp):
        r_t, w_t, k_t, v_t, kk_t, a_t = inp
        sa = jnp.einsum('bhij,bhj->bhi', s, -kk_t)
        s = (s * w_t[:, :, None, :] + sa[:, :, :, None] * (kk_t * a_t)[:, :, None, :]
             + v_t[:, :, :, None] * k_t[:, :, None, :])
        return s, jnp.einsum('bhij,bhj->bhi', s, r_t)

    xs = tuple(jnp.moveaxis(u, 1, 0) for u in (r, decay, k, v, kk, a))
    s_t, ys = lax.scan(step, s0.astype(f32), xs)
    y = layer_norm(jnp.moveaxis(ys, 0, 1), p['rw_gn_g'], p['rw_gn_b'], RW_GN_EPS)
    bonus = jnp.sum(r * k * p['rw_rk'].astype(f32), axis=-1, keepdims=True) * v
    out = (y + bonus).reshape(b, t, BRANCH).astype(za.dtype)
    return out, za[:, -1], s_t.astype(za.dtype)


def mla_project(zb, pos, p):
    b, t, _ = zb.shape
    qa = rms_norm(zb[..., :Q_RANK], p['mla_qa_g'])
    q = (qa @ p['mla_w_uq']).reshape(b, t, MLA_HEADS, QK_HD)
    q_nope = rms_norm(q[..., :NOPE], p['mla_q_norm_g'][:NOPE])
    q_rope = rope(rms_norm(q[..., NOPE:], p['mla_q_norm_g'][NOPE:]), pos)
    ckv = rms_norm(zb[..., Q_RANK:Q_RANK + KV_RANK], p['mla_kva_g'])
    krope = rope(rms_norm(zb[..., Q_RANK + KV_RANK:], p['mla_k_norm_g'][NOPE:]), pos)
    return q_nope, q_rope, ckv, krope


def mla_keys(ckv, p):
    b, n, _ = ckv.shape
    k_nope = rms_norm((ckv @ p['mla_w_uk']).reshape(b, n, MLA_HEADS, NOPE), p['mla_k_norm_g'][:NOPE])
    v = (ckv @ p['mla_w_uv']).reshape(b, n, MLA_HEADS, VHD)
    return k_nope, v


def mla_attend(q_nope, q_rope, qpos, k_nope, krope, v, kpos):
    s = (jnp.einsum('bqhd,bkhd->bhqk', q_nope, k_nope).astype(jnp.float32)
         + jnp.einsum('bqhd,bkd->bhqk', q_rope, krope).astype(jnp.float32)) * (1.0 / math.sqrt(QK_HD))
    mask = (qpos // CHUNK)[:, None] >= (kpos // CHUNK)[None, :]
    s = jnp.where(mask[None, None], s, -1e30)
    pr = jax.nn.softmax(s, axis=-1).astype(v.dtype)
    return jnp.einsum('bhqk,bkhd->bqhd', pr, v)


def mla_prompt(zb, p):
    b, t, _ = zb.shape
    pos = jnp.arange(t)
    q_nope, q_rope, ckv, krope = mla_project(zb, pos, p)
    k_nope, v = mla_keys(ckv, p)
    nb = t // MLA_QBLOCK
    blk = lambda u: jnp.moveaxis(u.reshape((b, nb, MLA_QBLOCK) + u.shape[2:]), 1, 0)
    o = lax.map(lambda a: mla_attend(a[0], a[1], a[2], k_nope, krope, v, pos),
                (blk(q_nope), blk(q_rope), pos.reshape(nb, MLA_QBLOCK)))
    o = jnp.moveaxis(o, 0, 1).reshape(b, t, BRANCH)
    return o, ckv, krope


def mla_sample(zb, cache_ckv, cache_krope, p):
    b, t, _ = zb.shape
    past = cache_ckv.shape[1]
    pos = past + jnp.arange(t)
    q_nope, q_rope, ckv, krope = mla_project(zb, pos, p)
    ckv_all = jnp.concatenate([cache_ckv.astype(ckv.dtype), ckv], axis=1)
    kr_all = jnp.concatenate([cache_krope.astype(krope.dtype), krope], axis=1)
    k_nope, v = mla_keys(ckv_all, p)
    o = mla_attend(q_nope, q_rope, pos, k_nope, kr_all, v, jnp.arange(past + t))
    return o.reshape(b, t, BRANCH), ckv, krope


def sgu_mix(zc, p):
    b, t, _ = zc.shape
    u = zc[..., :BRANCH]
    vn = layer_norm(zc[..., BRANCH:], p['sgu_ln_g'], p['sgu_ln_b'], SGU_EPS)
    nc = -(-t // SGU_CHUNK)
    tp = nc * SGU_CHUNK
    vp = jnp.pad(vn, ((0, 0), (0, tp - t), (0, 0))).reshape(b, nc, SGU_CHUNK, SGU_HEADS, SGU_HD)
    tri = jnp.tril(jnp.ones((SGU_CHUNK, SGU_CHUNK), dtype=bool))
    ws = jnp.where(tri[None], p['sgu_w'], 0.0).astype(vp.dtype)
    s = jnp.einsum('hij,bnjhc->bnihc', ws, vp) + jnp.transpose(p['sgu_b'])[None, None, :, :, None]
    s = s.reshape(b, tp, BRANCH)[:, :t]
    return u * s, vn


def pool_mix(zd, hist, p):
    b, t, _ = zd.shape
    xin = zd if hist is None else jnp.concatenate([hist.astype(zd.dtype), zd], axis=1)
    n_len = xin.shape[1]
    n_hist = n_len - t
    xf = xin.astype(jnp.float32).reshape(b, n_len, POOL_GROUPS, POOL_GD)
    c = jnp.concatenate([jnp.zeros((b, 1, POOL_GROUPS, POOL_GD), jnp.float32), jnp.cumsum(xf, axis=1)], axis=1)
    j = n_hist + jnp.arange(t)
    win = jnp.array(POOL_WINDOWS, dtype=jnp.int32)
    lo = jnp.maximum(j[:, None] + 1 - win[None, :], 0)
    ssum = c[:, j + 1] - c[:, lo, jnp.arange(POOL_GROUPS)[None, :]]
    cnt = jnp.minimum(j[:, None] + 1, win[None, :]).astype(jnp.float32)
    d = ssum / cnt[None, :, :, None] - xf[:, n_hist:]
    out = jnp.einsum('btgc,gcd->btgd', d, p['pool_w'].astype(jnp.float32))
    out = out.reshape(b, t, BRANCH) * p['pool_scale'].astype(jnp.float32)
    return out.astype(zd.dtype), xin[:, n_len - POOL_HIST:]


def trunk_layer(x, p, hist):
    b, t, _ = x.shape
    z = rms_norm(x, p['norm_g']) @ p['w_in']
    za, zb = z[..., OFF_A:OFF_B], z[..., OFF_B:OFF_C]
    zc, zd, gates = z[..., OFF_C:OFF_D], z[..., OFF_D:OFF_G], z[..., OFF_G:]
    if hist is None:
        shift0 = jnp.zeros((b, RW_SHIFT), x.dtype)
        s0 = jnp.zeros((b, RW_HEADS, RW_HD, RW_HD), jnp.float32)
        ya, shift_n, wkv_n = rwkv7_mix(za, shift0, s0, p)
        yb, ckv, kr = mla_prompt(zb, p)
        yd, pool_n = pool_mix(zd, None, p)
    else:
        c_ckv, c_kr, s_wkv, s_shift, s_pool = hist
        ya, shift_n, wkv_n = rwkv7_mix(za, s_shift, s_wkv, p)
        yb, ckv, kr = mla_sample(zb, c_ckv, c_kr, p)
        yd, pool_n = pool_mix(zd, s_pool, p)
    yc, vn = sgu_mix(zc, p)
    y = jnp.concatenate([ya, yb, yc.astype(x.dtype), yd], axis=-1) * jax.nn.silu(gates)
    out = x + y @ p['w_out']
    return out, (ckv, kr, wkv_n, shift_n, pool_n, vn)


def setup_inputs(seed: int = 0) -> dict:
    key = jax.random.key(seed)
    ks = iter(jax.random.split(key, 40))
    nrm = lambda shape, s: s * jax.random.normal(next(ks), shape, jnp.float32)
    one = lambda shape, s: 1.0 + s * jax.random.normal(next(ks), shape, jnp.float32)
    return {
        'x_prompt': nrm((BATCH, SEQ, D_MODEL), 1.0),
        'x_sample': nrm((DEC_BATCH, DEC_SEQ, D_MODEL), 1.0),
        'cache_ckv': nrm((DEPTH, DEC_BATCH, PAST_LEN, KV_RANK), 1.0),
        'cache_krope': nrm((DEPTH, DEC_BATCH, PAST_LEN, ROPE), 1.0),
        'state_wkv': nrm((DEPTH, DEC_BATCH, RW_HEADS, RW_HD, RW_HD), 1.0),
        'state_shift': nrm((DEPTH, DEC_BATCH, RW_SHIFT), 1.0),
        'state_pool': nrm((DEPTH, DEC_BATCH, POOL_HIST, BRANCH), 1.0),
        'norm_g': one((DEPTH, D_MODEL), 0.1),
        'w_in': nrm((DEPTH, D_MODEL, D_IN), D_MODEL ** -0.5),
        'w_out': nrm((DEPTH, D_MIX, D_MODEL), 0.5 * D_MIX ** -0.5),
        'rw_mu': jax.random.uniform(next(ks), (DEPTH, RW_SHIFT), jnp.float32, 0.1, 0.9),
        'rw_w0': nrm((DEPTH, BRANCH), 0.5),
        'rw_w2': nrm((DEPTH, RW_LORA_W, BRANCH), 0.5 * RW_LORA_W ** -0.5),
        'rw_a0': nrm((DEPTH, BRANCH), 0.5),
        'rw_a2': nrm((DEPTH, RW_LORA_A, BRANCH), 0.5 * RW_LORA_A ** -0.5),
        'rw_kk': one((DEPTH, BRANCH), 0.1),
        'rw_ka': one((DEPTH, BRANCH), 0.1),
        'rw_rk': nrm((DEPTH, RW_HEADS, RW_HD), 0.1),
        'rw_gn_g': one((DEPTH, RW_HEADS, RW_HD), 0.1),
        'rw_gn_b': nrm((DEPTH, RW_HEADS, RW_HD), 0.02),
        'mla_qa_g': one((DEPTH, Q_RANK), 0.1),
        'mla_w_uq': nrm((DEPTH, Q_RANK, MLA_HEADS * QK_HD), Q_RANK ** -0.5),
        'mla_kva_g': one((DEPTH, KV_RANK), 0.1),
        'mla_w_uk': nrm((DEPTH, KV_RANK, MLA_HEADS * NOPE), KV_RANK ** -0.5),
        'mla_w_uv': nrm((DEPTH, KV_RANK, MLA_HEADS * VHD), KV_RANK ** -0.5),
        'mla_q_norm_g': one((DEPTH, QK_HD), 0.1),
        'mla_k_norm_g': one((DEPTH, QK_HD), 0.1),
        'sgu_w': nrm((DEPTH, SGU_HEADS, SGU_CHUNK, SGU_CHUNK), 0.5 * SGU_CHUNK ** -0.5),
        'sgu_b': one((DEPTH, SGU_HEADS, SGU_CHUNK), 0.1),
        'sgu_ln_g': one((DEPTH, BRANCH), 0.1),
        'sgu_ln_b': nrm((DEPTH, BRANCH), 0.02),
        'pool_w': nrm((DEPTH, POOL_GROUPS, POOL_GD, POOL_GD), POOL_GD ** -0.5),
        'pool_scale': one((DEPTH, BRANCH), 0.1),
    }


def reference(x_prompt, x_sample, cache_ckv, cache_krope, state_wkv, state_shift, state_pool,
              norm_g, w_in, w_out, rw_mu, rw_w0, rw_w2, rw_a0, rw_a2, rw_kk, rw_ka, rw_rk,
              rw_gn_g, rw_gn_b, mla_qa_g, mla_w_uq, mla_kva_g, mla_w_uk, mla_w_uv,
              mla_q_norm_g, mla_k_norm_g, sgu_w, sgu_b, sgu_ln_g, sgu_ln_b, pool_w, pool_scale):
    yp, ys = x_prompt, x_sample
    new_p = [[] for _ in range(5)]
    new_s = [[] for _ in range(6)]
    for l in range(DEPTH):
        p = {
            'norm_g': norm_g[l], 'w_in': w_in[l], 'w_out': w_out[l],
            'rw_mu': rw_mu[l], 'rw_w0': rw_w0[l], 'rw_w2': rw_w2[l], 'rw_a0': rw_a0[l],
            'rw_a2': rw_a2[l], 'rw_kk': rw_kk[l], 'rw_ka': rw_ka[l], 'rw_rk': rw_rk[l],
            'rw_gn_g': rw_gn_g[l], 'rw_gn_b': rw_gn_b[l],
            'mla_qa_g': mla_qa_g[l], 'mla_w_uq': mla_w_uq[l], 'mla_kva_g': mla_kva_g[l],
            'mla_w_uk': mla_w_uk[l], 'mla_w_uv': mla_w_uv[l],
            'mla_q_norm_g': mla_q_norm_g[l], 'mla_k_norm_g': mla_k_norm_g[l],
            'sgu_w': sgu_w[l], 'sgu_b': sgu_b[l], 'sgu_ln_g': sgu_ln_g[l], 'sgu_ln_b': sgu_ln_b[l],
            'pool_w': pool_w[l], 'pool_scale': pool_scale[l],
        }
        yp, st_p = trunk_layer(yp, p, None)
        ys, st_s = trunk_layer(ys, p, (cache_ckv[l], cache_krope[l], state_wkv[l], state_shift[l], state_pool[l]))
        for lst, arr in zip(new_p, st_p[:5]):
            lst.append(arr)
        for lst, arr in zip(new_s, st_s):
            lst.append(arr)
    ckv_p, kr_p, wkv_p, shift_p, pool_p = [jnp.stack(a, axis=0) for a in new_p]
    ckv_s, kr_s, wkv_s, shift_s, pool_s, sgu_v_s = [jnp.stack(a, axis=0) for a in new_s]
    return (yp, ys, ckv_p, kr_p, wkv_p, shift_p, pool_p, ckv_s, kr_s, wkv_s, shift_s, pool_s, sgu_v_s)
```

```python
import functools
import math

import jax
import jax.numpy as jnp
import numpy as np
from jax import lax
from jax.experimental import pallas as pl
from jax.experimental.pallas import tpu as pltpu

F32 = jnp.float32
BF16 = jnp.bfloat16
HI = lax.Precision.HIGHEST

D_MODEL = 1024
CHUNK = 64
BRANCH = 256
NORM_EPS = 1e-6
HEADS = 4
HD = 64
RW_LORA = 64
RW_SHIFT = 3 * BRANCH + 2 * RW_LORA
RW_GN_EPS = 64e-5
NOPE = 64
ROPE = 32
QK_HD = NOPE + ROPE
Q_RANK = 192
KV_RANK = 128
ROPE_THETA = 10000.0
SGU_CHUNK = 128
SGU_EPS = 1e-5
POOL_WINDOWS = (2, 4, 8, 16)
POOL_HIST = 15
D_IN = 3040

LANE = 128
HALO = 16
Q_PAD = 256
HEAD_PAD = 128
VMEM_LIMIT = 48 * 1024 * 1024

SEG_A = (0, 896)
SEG_B = (896, 1408)
SEG_C = (1408, 1920)
SEG_D = (1920, 2176)
SEG_G = (2176, 3200)
D_IN_PAD = 3200


def _params(sem):
    return pltpu.CompilerParams(dimension_semantics=sem, vmem_limit_bytes=VMEM_LIMIT)


def _full(shape):
    nd = len(shape)
    return pl.BlockSpec(shape, lambda *_: (0,) * nd)


def _inproj_kernel(x_ref, g_ref, w_ref, za_ref, zb_ref, zc_ref, zd_ref, zg_ref):
    x = x_ref[...]
    ms = jnp.mean(x * x, axis=-1, keepdims=True)
    xn = (x * lax.rsqrt(ms + NORM_EPS) * g_ref[...]).astype(BF16)
    for ref, (lo, hi) in ((za_ref, SEG_A), (zb_ref, SEG_B), (zc_ref, SEG_C), (zd_ref, SEG_D), (zg_ref, SEG_G)):
        ref[...] = jnp.dot(xn, w_ref[:, lo:hi], preferred_element_type=F32)


def _inproj(x2d, g, w):
    n = x2d.shape[0]
    tm = min(512, n)
    widths = [hi - lo for lo, hi in (SEG_A, SEG_B, SEG_C, SEG_D, SEG_G)]
    return pl.pallas_call(
        _inproj_kernel,
        grid=(n // tm,),
        in_specs=[pl.BlockSpec((tm, D_MODEL), lambda i: (i, 0)), _full((1, D_MODEL)), _full((D_MODEL, D_IN_PAD))],
        out_specs=[pl.BlockSpec((tm, wd), lambda i: (i, 0)) for wd in widths],
        out_shape=[jax.ShapeDtypeStruct((n, wd), F32) for wd in widths],
        compiler_params=_params(("parallel",)),
        name="inproj",
    )(x2d, g, w)


def _softplus(x):
    return jnp.maximum(x, 0.0) + jnp.log(1.0 + jnp.exp(-jnp.abs(x)))


def _dot_hi(a, b):
    return jnp.dot(a, b, preferred_element_type=F32, precision=HI)


def _dot_nt_hi(a, b):
    return lax.dot_general(a, b, (((1,), (1,)), ((), ())), preferred_element_type=F32, precision=HI)


def _rwkv_kernel(za_ref, shift_ref, z0_ref, mu_ref, wl_ref, w0_ref, a0_ref, kkg_ref, kag_ref, rkg_ref,
                 gng_ref, gnb_ref, ya_ref, zfin_ref, z_scr, carry_scr, *, c, levels):
    ci = pl.program_id(1)
    hc = HEADS * c

    @pl.when(ci == 0)
    def _():
        z_scr[...] = z0_ref[0]
        carry_scr[...] = jnp.broadcast_to(shift_ref[0], carry_scr.shape)

    za = za_ref[0]
    row = lax.broadcasted_iota(jnp.int32, za.shape, 0)
    prev = jnp.where(row == 0, carry_scr[0:1, :], pltpu.roll(za, 1, 0))
    carry_scr[...] = jnp.broadcast_to(za[c - 1:c, :], carry_scr.shape)
    zs = za + mu_ref[...] * (prev - za)
    r = zs[:, 0:BRANCH]
    k = zs[:, BRANCH:2 * BRANCH]
    v = zs[:, 2 * BRANCH:3 * BRANCH]
    wa = zs[:, 3 * BRANCH:RW_SHIFT]
    lane128 = lax.broadcasted_iota(jnp.int32, wa.shape, 1)
    lora_in = jnp.where(lane128 < RW_LORA, jnp.tanh(wa), wa).astype(BF16)
    lora = jnp.dot(lora_in, wl_ref[...], preferred_element_type=F32)
    w_log = -_softplus(-(w0_ref[...] + lora[:, :BRANCH])) - 0.5
    logd = -jnp.exp(w_log)
    asig = 1.0 / (1.0 + jnp.exp(-(a0_ref[...] + lora[:, BRANCH:])))

    lane = lax.broadcasted_iota(jnp.int32, (1, BRANCH), 1)
    rr = lax.broadcasted_iota(jnp.int32, (BRANCH, BRANCH), 0)
    cc = lax.broadcasted_iota(jnp.int32, (BRANCH, BRANCH), 1)
    ones_bd = jnp.where(rr // HD == cc // HD, 1.0, 0.0).astype(F32)

    kkv = k * kkg_ref[...]
    kk = kkv * lax.rsqrt(jnp.maximum(_dot_hi(kkv * kkv, ones_bd), 1e-24))
    kmod = k * (1.0 + (asig - 1.0) * kag_ref[...])
    avec = -kk
    bvec = kk * asig

    tr = lax.broadcasted_iota(jnp.int32, (c, c), 0)
    tc = lax.broadcasted_iota(jnp.int32, (c, c), 1)
    tril = jnp.where(tc <= tr, 1.0, 0.0).astype(F32)
    lcum = _dot_hi(tril, logd)
    lexc = lcum - logd
    ltot = lcum[c - 1:c, :]
    g_in = jnp.exp(lcum)
    g_ex = jnp.exp(lexc)
    g_inv = jnp.exp(-lcum)
    g_rem = jnp.exp(ltot - lcum)

    def stack(x):
        return jnp.concatenate([jnp.where(lane // HD == h, x, 0.0) for h in range(HEADS)], axis=0)

    a_s = stack(avec * g_ex)
    r_s = stack(r * g_in)
    b_s = stack(bvec * g_inv)
    k_s = stack(kmod * g_inv)
    bq_s = stack(bvec * g_rem)
    kq_s = stack(kmod * g_rem)
    v_s = stack(v)

    sr = lax.broadcasted_iota(jnp.int32, (hc, hc), 0)
    sc = lax.broadcasted_iota(jnp.int32, (hc, hc), 1)
    same = (sr // c) == (sc // c)
    strict = lambda x: jnp.where(same, jnp.where(sc < sr, x, 0.0), 0.0)
    incl = lambda x: jnp.where(same, jnp.where(sc <= sr, x, 0.0), 0.0)
    a_ab = strict(_dot_nt_hi(a_s, b_s))
    a_ak = strict(_dot_nt_hi(a_s, k_s))
    a_rb = incl(_dot_nt_hi(r_s, b_s))
    a_rk = incl(_dot_nt_hi(r_s, k_s))

    eye = jnp.where(sr == sc, 1.0, 0.0).astype(F32)
    t_inv = eye + a_ab
    pw = a_ab
    for _ in range(levels - 1):
        pw = _dot_hi(pw, pw)
        t_inv = t_inv + _dot_hi(t_inv, pw)

    z = z_scr[...]
    u = _dot_hi(t_inv, _dot_hi(a_s, z) + _dot_hi(a_ak, v_s))
    y_s = _dot_hi(r_s, z) + _dot_hi(a_rb, u) + _dot_hi(a_rk, v_s)
    gdiag = jnp.where(rr == cc, jnp.exp(jnp.broadcast_to(ltot, (BRANCH, BRANCH))), 0.0)
    z_new = _dot_hi(gdiag, z) + _dot_hi(bq_s.T, u) + _dot_hi(kq_s.T, v_s)
    z_scr[...] = z_new

    y = y_s[0:c]
    for h in range(1, HEADS):
        y = y + y_s[h * c:(h + 1) * c]

    mean = _dot_hi(y, ones_bd) * (1.0 / HD)
    yc = y - mean
    var = _dot_hi(yc * yc, ones_bd) * (1.0 / HD)
    yn = yc * lax.rsqrt(var + RW_GN_EPS) * gng_ref[...] + gnb_ref[...]
    bonus = _dot_hi(r * kmod * rkg_ref[...], ones_bd) * v
    ya_ref[0] = yn + bonus

    @pl.when(ci == pl.num_programs(1) - 1)
    def _():
        zfin_ref[0] = z_new


def _rwkv(za, shift_prev, zbd0, lp):
    b, t, _ = za.shape
    c = min(CHUNK, t)
    levels = int(math.log2(c))
    row = lambda w: _full((1, w))
    return pl.pallas_call(
        functools.partial(_rwkv_kernel, c=c, levels=levels),
        grid=(b, t // c),
        in_specs=[pl.BlockSpec((1, c, RW_SHIFT), lambda i, j: (i, j, 0)),
                  pl.BlockSpec((1, 1, RW_SHIFT), lambda i, j: (i, 0, 0)),
                  pl.BlockSpec((1, BRANCH, BRANCH), lambda i, j: (i, 0, 0)),
                  row(RW_SHIFT), _full((2 * RW_LORA, 2 * BRANCH)), row(BRANCH), row(BRANCH), row(BRANCH),
                  row(BRANCH), row(BRANCH), row(BRANCH), row(BRANCH)],
        out_specs=[pl.BlockSpec((1, c, BRANCH), lambda i, j: (i, j, 0)),
                   pl.BlockSpec((1, BRANCH, BRANCH), lambda i, j: (i, 0, 0))],
        out_shape=[jax.ShapeDtypeStruct((b, t, BRANCH), F32), jax.ShapeDtypeStruct((b, BRANCH, BRANCH), F32)],
        scratch_shapes=[pltpu.VMEM((BRANCH, BRANCH), F32), pltpu.VMEM((8, RW_SHIFT), F32)],
        compiler_params=_params(("parallel", "arbitrary")),
        name="rwkv",
    )(za, shift_prev, zbd0, lp["rw_mu"], lp["rw_lora"], lp["rw_w0"], lp["rw_a0"], lp["rw_kk"], lp["rw_ka"],
      lp["rw_rk"], lp["rw_gn_g"], lp["rw_gn_b"])


def _rope_apply(x, cos, s1, s2):
    half = ROPE // 2
    return x * cos + pltpu.roll(x, LANE - half, 1) * s1 + pltpu.roll(x, half, 1) * s2


def _mla_q_kernel(zb_ref, cq_ref, s1q_ref, s2q_ref, ck_ref, s1k_ref, s2k_ref, gqa_ref, wuq_ref, gqn_ref,
                  gkva_ref, gkr_ref, q_ref, ckv_ref, kr_ref):
    z = zb_ref[0]
    zq = z[:, 0:Q_PAD]
    qa = zq * lax.rsqrt(jnp.sum(zq * zq, axis=-1, keepdims=True) * (1.0 / Q_RANK) + NORM_EPS) * gqa_ref[...]
    q = jnp.dot(qa.astype(BF16), wuq_ref[...], preferred_element_type=F32)
    lane = lax.broadcasted_iota(jnp.int32, (1, HEAD_PAD), 1)
    is_nope = lane < NOPE
    sm_scale = 1.0 / math.sqrt(QK_HD)
    for h in range(HEADS):
        qh = q[:, h * HEAD_PAD:(h + 1) * HEAD_PAD]
        sq = qh * qh
        sn = jnp.sum(jnp.where(is_nope, sq, 0.0), axis=-1, keepdims=True) * (1.0 / NOPE)
        sq_rope = jnp.where(is_nope, 0.0, jnp.where(lane < QK_HD, sq, 0.0))
        sr = jnp.sum(sq_rope, axis=-1, keepdims=True) * (1.0 / ROPE)
        scale = jnp.where(is_nope, lax.rsqrt(sn + NORM_EPS), lax.rsqrt(sr + NORM_EPS))
        xr = qh * scale * gqn_ref[...]
        out = _rope_apply(xr, cq_ref[...], s1q_ref[...], s2q_ref[...])
        q_ref[0, :, h * HEAD_PAD:(h + 1) * HEAD_PAD] = (out * sm_scale).astype(BF16)
    zkv = z[:, Q_PAD:Q_PAD + KV_RANK]
    ckv_ref[0] = zkv * lax.rsqrt(jnp.mean(zkv * zkv, axis=-1, keepdims=True) + NORM_EPS) * gkva_ref[...]
    zkr = z[:, Q_PAD + KV_RANK:]
    xr = zkr * lax.rsqrt(jnp.sum(zkr * zkr, axis=-1, keepdims=True) * (1.0 / ROPE) + NORM_EPS) * gkr_ref[...]
    kr_ref[0] = _rope_apply(xr, ck_ref[...], s1k_ref[...], s2k_ref[...])


def _mla_q(zb, tabs, lp):
    b, t, _ = zb.shape
    tm = min(512, t)
    tab = pl.BlockSpec((tm, LANE), lambda i, j: (j, 0))
    row = lambda w: _full((1, w))
    return pl.pallas_call(
        _mla_q_kernel,
        grid=(b, t // tm),
        in_specs=[pl.BlockSpec((1, tm, 512), lambda i, j: (i, j, 0)), tab, tab, tab, tab, tab, tab,
                  row(Q_PAD), _full((Q_PAD, HEADS * HEAD_PAD)), row(HEAD_PAD), row(KV_RANK), row(LANE)],
        out_specs=[pl.BlockSpec((1, tm, HEADS * HEAD_PAD), lambda i, j: (i, j, 0)),
                   pl.BlockSpec((1, tm, KV_RANK), lambda i, j: (i, j, 0)),
                   pl.BlockSpec((1, tm, LANE), lambda i, j: (i, j, 0))],
        out_shape=[jax.ShapeDtypeStruct((b, t, HEADS * HEAD_PAD), BF16),
                   jax.ShapeDtypeStruct((b, t, KV_RANK), F32),
                   jax.ShapeDtypeStruct((b, t, LANE), F32)],
        compiler_params=_params(("parallel", "parallel")),
        name="mla_q",
    )(zb, *tabs, lp["mla_qa_g"], lp["mla_w_uq"], lp["mla_qn_g"], lp["mla_kva_g"], lp["mla_kr_g"])


def _mla_kv_kernel(ckv_ref, kr_ref, wuk_ref, gkn_ref, wuv_ref, k_ref, v_ref):
    cb = ckv_ref[0].astype(BF16)
    kn = jnp.dot(cb, wuk_ref[...], preferred_element_type=F32)
    krs = pltpu.roll(kr_ref[0], NOPE, 1)
    for h in range(HEADS):
        kh = kn[:, h * HEAD_PAD:(h + 1) * HEAD_PAD]
        ms = jnp.sum(kh * kh, axis=-1, keepdims=True) * (1.0 / NOPE)
        k_ref[0, :, h * HEAD_PAD:(h + 1) * HEAD_PAD] = (kh * lax.rsqrt(ms + NORM_EPS) * gkn_ref[...] + krs).astype(BF16)
    v_ref[0] = jnp.dot(cb, wuv_ref[...], preferred_element_type=F32).astype(BF16)


def _mla_kv(ckv, kr, lp):
    b, n, _ = ckv.shape
    tm = 512 if n % 512 == 0 else n
    wide = HEADS * HEAD_PAD
    return pl.pallas_call(
        _mla_kv_kernel,
        grid=(b, n // tm),
        in_specs=[pl.BlockSpec((1, tm, KV_RANK), lambda i, j: (i, j, 0)),
                  pl.BlockSpec((1, tm, LANE), lambda i, j: (i, j, 0)),
                  _full((KV_RANK, wide)), _full((1, HEAD_PAD)), _full((KV_RANK, wide))],
        out_specs=[pl.BlockSpec((1, tm, wide), lambda i, j: (i, j, 0)),
                   pl.BlockSpec((1, tm, wide), lambda i, j: (i, j, 0))],
        out_shape=[jax.ShapeDtypeStruct((b, n, wide), BF16), jax.ShapeDtypeStruct((b, n, wide), BF16)],
        compiler_params=_params(("parallel", "parallel")),
        name="mla_kv",
    )(ckv, kr, lp["mla_w_uk"], lp["mla_kn_g"], lp["mla_w_uv"])


def _flash_kernel(qt_ref, kt_ref, q_ref, k_ref, v_ref, o_ref, m_scr, l_scr, acc_scr, *, tq, tk, causal, kv_valid):
    p_id = pl.program_id(1)
    qi = qt_ref[p_id]
    ki = kt_ref[p_id]

    @pl.when(ki == 0)
    def _():
        m_scr[...] = jnp.full(m_scr.shape, -1e30, F32)
        l_scr[...] = jnp.zeros(l_scr.shape, F32)
        acc_scr[...] = jnp.zeros(acc_scr.shape, F32)

    def update(mask):
        for h in range(HEADS):
            sl = slice(h * HEAD_PAD, (h + 1) * HEAD_PAD)
            s = lax.dot_general(q_ref[0, :, sl], k_ref[0, :, sl], (((1,), (1,)), ((), ())),
                                preferred_element_type=F32)
            if mask is not None:
                s = jnp.where(mask, s, -1e30)
            m_prev = m_scr[h]
            m_new = jnp.maximum(m_prev, jnp.max(s, axis=-1, keepdims=True))
            alpha = jnp.exp(m_prev - m_new)
            p = jnp.exp(s - pltpu.repeat(m_new, tk // LANE, 1))
            l_scr[h] = alpha * l_scr[h] + jnp.sum(p, axis=-1, keepdims=True)
            m_scr[h] = m_new
            acc_scr[h] = alpha * acc_scr[h] + jnp.dot(p.astype(BF16), v_ref[0, :, sl], preferred_element_type=F32)

    if causal:
        @pl.when(ki < qi)
        def _():
            update(None)

        @pl.when(ki == qi)
        def _():
            r = lax.broadcasted_iota(jnp.int32, (tq, tk), 0)
            c = lax.broadcasted_iota(jnp.int32, (tq, tk), 1)
            update((r // CHUNK) >= (c // CHUNK))
        is_last = ki == qi
    else:
        c = lax.broadcasted_iota(jnp.int32, (tq, tk), 1)
        update(c < kv_valid)
        is_last = ki == 0

    @pl.when(is_last)
    def _():
        outs = [acc_scr[h] / l_scr[h] for h in range(HEADS)]
        o_ref[0, :, 0:LANE] = outs[0] + pltpu.roll(outs[1], HD, 1)
        o_ref[0, :, LANE:2 * LANE] = outs[2] + pltpu.roll(outs[3], HD, 1)


def _flash(q, k, v, *, causal, kv_valid):
    b, t, wide = q.shape
    n = k.shape[1]
    if causal:
        tq = tk = min(512, t)
        pairs = [(i, j) for i in range(t // tq) for j in range(i + 1)]
    else:
        tq, tk = t, n
        pairs = [(0, 0)]
    qt = jnp.asarray(np.array([p[0] for p in pairs], np.int32))
    kt = jnp.asarray(np.array([p[1] for p in pairs], np.int32))
    grid_spec = pltpu.PrefetchScalarGridSpec(
        num_scalar_prefetch=2,
        grid=(b, len(pairs)),
        in_specs=[pl.BlockSpec((1, tq, wide), lambda i, p, qt, kt: (i, qt[p], 0)),
                  pl.BlockSpec((1, tk, wide), lambda i, p, qt, kt: (i, kt[p], 0)),
                  pl.BlockSpec((1, tk, wide), lambda i, p, qt, kt: (i, kt[p], 0))],
        out_specs=pl.BlockSpec((1, tq, BRANCH), lambda i, p, qt, kt: (i, qt[p], 0)),
        scratch_shapes=[pltpu.VMEM((HEADS, tq, LANE), F32), pltpu.VMEM((HEADS, tq, LANE), F32),
                        pltpu.VMEM((HEADS, tq, HEAD_PAD), F32)],
    )
    return pl.pallas_call(
        functools.partial(_flash_kernel, tq=tq, tk=tk, causal=causal, kv_valid=kv_valid),
        grid_spec=grid_spec,
        out_shape=jax.ShapeDtypeStruct((b, t, BRANCH), F32),
        compiler_params=_params(("parallel", "arbitrary")),
        name="flash",
    )(qt, kt, q, k, v)


def _combine_kernel(x_ref, ya_ref, yb_ref, zc_ref, zd_ref, halo_ref, zg_ref, lng_ref, lnb_ref, ws_ref, sb_ref,
                    pw_ref, psc_ref, wout_ref, *out_refs, tm, cs, n_hist, halo_is_prev_tile):
    out_ref = out_refs[0]
    ti = pl.program_id(1)
    lane = lax.broadcasted_iota(jnp.int32, (1, BRANCH), 1)

    zc = zc_ref[0]
    u = zc[:, 0:BRANCH]
    vraw = zc[:, BRANCH:]
    xc = vraw - jnp.mean(vraw, axis=-1, keepdims=True)
    var = jnp.mean(xc * xc, axis=-1, keepdims=True)
    vn = xc * lax.rsqrt(var + SGU_EPS) * lng_ref[...] + lnb_ref[...]
    if len(out_refs) > 1:
        out_refs[1][0] = vn
    vnb = vn.astype(BF16)
    zero = jnp.zeros((), BF16)
    parts = []
    for n in range(tm // cs):
        vch = vnb[n * cs:(n + 1) * cs]
        stacked = jnp.concatenate([jnp.where(lane // HD == h, vch, zero) for h in range(HEADS)], axis=0)
        parts.append(jnp.dot(ws_ref[...], stacked, preferred_element_type=F32) + sb_ref[...])
    yc = u * (parts[0] if len(parts) == 1 else jnp.concatenate(parts, axis=0))

    zd = zd_ref[0]
    halo = halo_ref[0]
    if halo_is_prev_tile:
        halo = jnp.where(ti == 0, 0.0, halo)
    ext = jnp.concatenate([halo, zd], axis=0)
    sums = []
    acc = ext
    for sh in (1, 2, 4, 8):
        acc = acc + pltpu.roll(acc, sh, 0)
        sums.append(acc[HALO:])
    t_idx = lax.broadcasted_iota(jnp.int32, (tm, BRANCH), 0) + (ti * tm + n_hist + 1)
    lane2 = lax.broadcasted_iota(jnp.int32, (tm, BRANCH), 1)
    win = jnp.where(lane2 < HD, POOL_WINDOWS[0],
                    jnp.where(lane2 < 2 * HD, POOL_WINDOWS[1],
                              jnp.where(lane2 < 3 * HD, POOL_WINDOWS[2], POOL_WINDOWS[3])))
    cnt = jnp.minimum(t_idx, win).astype(F32)
    ssum = jnp.where(lane2 < HD, sums[0],
                     jnp.where(lane2 < 2 * HD, sums[1], jnp.where(lane2 < 3 * HD, sums[2], sums[3])))
    d = ssum / cnt - zd
    yd = jnp.dot(d.astype(BF16), pw_ref[...], preferred_element_type=F32) * psc_ref[...]

    g = zg_ref[0]
    gate = g / (1.0 + jnp.exp(-g))
    y = jnp.concatenate([ya_ref[0], yb_ref[0], yc, yd], axis=-1) * gate
    out_ref[0] = x_ref[0] + jnp.dot(y.astype(BF16), wout_ref[...], preferred_element_type=F32)


def _combine(x, ya, yb, zc, zd, halo, zg, lp, *, n_hist, want_vn):
    b, t, _ = x.shape
    tm = min(512, t)
    cs = min(SGU_CHUNK, t)
    halo_is_prev_tile = halo is None
    if halo_is_prev_tile:
        halo = zd
        per = tm // HALO
        halo_spec = pl.BlockSpec((1, HALO, BRANCH), lambda i, j: (i, jnp.maximum(j * per - 1, 0), 0))
    else:
        halo_spec = pl.BlockSpec((1, HALO, BRANCH), lambda i, j: (i, 0, 0))
    blk = lambda w: pl.BlockSpec((1, tm, w), lambda i, j: (i, j, 0))
    row = lambda w: _full((1, w))
    out_specs = [blk(D_MODEL)]
    out_shape = [jax.ShapeDtypeStruct((b, t, D_MODEL), F32)]
    if want_vn:
        out_specs.append(blk(BRANCH))
        out_shape.append(jax.ShapeDtypeStruct((b, t, BRANCH), F32))
    res = pl.pallas_call(
        functools.partial(_combine_kernel, tm=tm, cs=cs, n_hist=n_hist, halo_is_prev_tile=halo_is_prev_tile),
        grid=(b, t // tm),
        in_specs=[blk(D_MODEL), blk(BRANCH), blk(BRANCH), blk(2 * BRANCH), blk(BRANCH), halo_spec, blk(D_MODEL),
                  row(BRANCH), row(BRANCH), _full((cs, HEADS * cs)), _full((cs, BRANCH)),
                  _full((BRANCH, BRANCH)), row(BRANCH), _full((D_MODEL, D_MODEL))],
        out_specs=out_specs,
        out_shape=out_shape,
        compiler_params=_params(("parallel", "parallel")),
        name="combine",
    )(x, ya, yb, zc, zd, halo, zg, lp["sgu_ln_g"], lp["sgu_ln_b"], lp["sgu_w_cat"][cs], lp["sgu_b_tab"][cs],
      lp["pool_w_bd"], lp["pool_scale"], lp["w_out"])
    return res


def _pad_cols(w, total):
    return jnp.pad(w, ((0, 0), (0, total - w.shape[1])))


def _head_slots(w, real):
    rows = w.shape[0]
    w = w.reshape(rows, HEADS, real)
    return jnp.pad(w, ((0, 0), (0, 0), (0, HEAD_PAD - real))).reshape(rows, HEADS * HEAD_PAD)


def _layer_params(l, norm_g, w_in, w_out, rw_mu, rw_w0, rw_w2, rw_a0, rw_a2, rw_kk, rw_ka, rw_rk, rw_gn_g,
                  rw_gn_b, mla_qa_g, mla_w_uq, mla_kva_g, mla_w_uk, mla_w_uv, mla_q_norm_g, mla_k_norm_g, sgu_w,
                  sgu_b, sgu_ln_g, sgu_ln_b, pool_w, pool_scale, sgu_sizes):
    wi = w_in[l]
    off_b = RW_SHIFT
    off_c = off_b + Q_RANK + KV_RANK + ROPE
    off_d = off_c + 2 * BRANCH
    off_g = off_d + BRANCH
    w_in_r = jnp.concatenate([
        wi[:, :off_b],
        _pad_cols(wi[:, off_b:off_b + Q_RANK], Q_PAD),
        wi[:, off_b + Q_RANK:off_b + Q_RANK + KV_RANK],
        _pad_cols(wi[:, off_b + Q_RANK + KV_RANK:off_c], LANE),
        wi[:, off_c:off_d], wi[:, off_d:off_g], wi[:, off_g:]], axis=1).astype(BF16)
    zeros = jnp.zeros((RW_LORA, BRANCH), F32)
    rw_lora = jnp.concatenate([jnp.concatenate([rw_w2[l], zeros], 1), jnp.concatenate([zeros, rw_a2[l]], 1)], 0)
    row = lambda v: v.reshape(1, -1).astype(F32)
    qn = mla_q_norm_g[l]
    kn = mla_k_norm_g[l]
    tri = jnp.tril(jnp.ones((SGU_CHUNK, SGU_CHUNK), bool))
    ws = jnp.where(tri[None], sgu_w[l], 0.0)
    sgu_w_cat = {cs: jnp.concatenate([ws[h, :cs, :cs] for h in range(HEADS)], axis=1).astype(BF16)
                 for cs in sgu_sizes}
    sgu_b_tab = {cs: jnp.repeat(jnp.transpose(sgu_b[l])[:cs], HD, axis=1) for cs in sgu_sizes}
    eye = jnp.eye(HEADS, dtype=F32)
    pool_w_bd = jnp.einsum("gcd,gh->gchd", pool_w[l], eye).reshape(BRANCH, BRANCH).astype(BF16)
    return {
        "norm_g": row(norm_g[l]), "w_in": w_in_r, "w_out": w_out[l].astype(BF16),
        "rw_mu": row(rw_mu[l]), "rw_lora": rw_lora.astype(BF16), "rw_w0": row(rw_w0[l]), "rw_a0": row(rw_a0[l]),
        "rw_kk": row(rw_kk[l]), "rw_ka": row(rw_ka[l]), "rw_rk": row(rw_rk[l]), "rw_gn_g": row(rw_gn_g[l]),
        "rw_gn_b": row(rw_gn_b[l]),
        "mla_qa_g": _pad_cols(row(mla_qa_g[l]), Q_PAD),
        "mla_w_uq": jnp.pad(_head_slots(mla_w_uq[l], QK_HD), ((0, Q_PAD - Q_RANK), (0, 0))).astype(BF16),
        "mla_qn_g": _pad_cols(row(qn), HEAD_PAD),
        "mla_kva_g": row(mla_kva_g[l]),
        "mla_kr_g": _pad_cols(row(kn[NOPE:]), LANE),
        "mla_w_uk": _head_slots(mla_w_uk[l], NOPE).astype(BF16),
        "mla_kn_g": _pad_cols(row(kn[:NOPE]), HEAD_PAD),
        "mla_w_uv": _head_slots(mla_w_uv[l], HD).astype(BF16),
        "sgu_ln_g": row(sgu_ln_g[l]), "sgu_ln_b": row(sgu_ln_b[l]), "sgu_w_cat": sgu_w_cat, "sgu_b_tab": sgu_b_tab,
        "pool_w_bd": pool_w_bd, "pool_scale": row(pool_scale[l]),
    }


def _rope_tables(pos):
    half = ROPE // 2
    inv = ROPE_THETA ** (-jnp.arange(half, dtype=F32) / half)
    ang = pos.astype(F32)[:, None] * inv[None, :]
    cos, sin = jnp.cos(ang), jnp.sin(ang)
    t = pos.shape[0]

    def place(first, second, off, fill):
        base = jnp.full((t, LANE), fill, F32)
        base = lax.dynamic_update_slice(base, first, (0, off))
        return lax.dynamic_update_slice(base, second, (0, off + half))

    z = jnp.zeros_like(sin)
    tabs = []
    for off, fill in ((NOPE, 1.0), (0, 0.0)):
        tabs += [place(cos, cos, off, fill), place(-sin, z, off, 0.0), place(z, sin, off, 0.0)]
    return tabs


def _state_to_blockdiag(s):
    b = s.shape[0]
    eye = jnp.eye(HEADS, dtype=s.dtype)
    return jnp.einsum("bhij,hg->bhjgi", s, eye).reshape(b, BRANCH, BRANCH)


def _blockdiag_to_state(z):
    b = z.shape[0]
    z5 = z.reshape(b, HEADS, HD, HEADS, HD)
    return jnp.stack([jnp.swapaxes(z5[:, h, :, h, :], 1, 2) for h in range(HEADS)], axis=1)


def _layer(x, lp, tabs, hist):
    b, t, _ = x.shape
    za, zb, zc, zd, zg = _inproj(x.reshape(b * t, D_MODEL), lp["norm_g"], lp["w_in"])
    za = za.reshape(b, t, RW_SHIFT)
    zb = zb.reshape(b, t, 512)
    zc = zc.reshape(b, t, 2 * BRANCH)
    zd = zd.reshape(b, t, BRANCH)
    zg = zg.reshape(b, t, D_MODEL)
    if hist is None:
        shift_prev = jnp.zeros((b, 1, RW_SHIFT), F32)
        zbd0 = jnp.zeros((b, BRANCH, BRANCH), F32)
    else:
        c_ckv, c_kr, s_wkv, s_shift, s_pool = hist
        shift_prev = s_shift[:, None, :]
        zbd0 = _state_to_blockdiag(s_wkv)
    ya, zfin = _rwkv(za, shift_prev, zbd0, lp)
    q, ckv, krp = _mla_q(zb, tabs, lp)
    if hist is None:
        k, v = _mla_kv(ckv, krp, lp)
        yb = _flash(q, k, v, causal=True, kv_valid=None)
        halo, n_hist = None, 0
        pool_src = zd
    else:
        past = c_ckv.shape[1]
        n_all = past + t
        n_pad = -(-n_all // LANE) * LANE
        ckv_all = jnp.pad(jnp.concatenate([c_ckv, ckv], axis=1), ((0, 0), (0, n_pad - n_all), (0, 0)))
        kr_all = jnp.concatenate([jnp.pad(c_kr, ((0, 0), (0, 0), (0, LANE - ROPE))), krp], axis=1)
        kr_all = jnp.pad(kr_all, ((0, 0), (0, n_pad - n_all), (0, 0)))
        k, v = _mla_kv(ckv_all, kr_all, lp)
        yb = _flash(q, k, v, causal=False, kv_valid=n_all)
        halo = jnp.pad(s_pool, ((0, 0), (HALO - POOL_HIST, 0), (0, 0)))
        n_hist = POOL_HIST
        pool_src = jnp.concatenate([s_pool, zd], axis=1)
    res = _combine(x, ya, yb, zc, zd, halo, zg, lp, n_hist=n_hist, want_vn=hist is not None)
    state = (ckv, krp[..., :ROPE], _blockdiag_to_state(zfin), za[:, -1], pool_src[:, pool_src.shape[1] - POOL_HIST:])
    if hist is None:
        return res[0], state
    return res[0], state + (res[1],)


def kernel(x_prompt, x_sample, cache_ckv, cache_krope, state_wkv, state_shift, state_pool, norm_g, w_in, w_out, rw_mu, rw_w0, rw_w2, rw_a0, rw_a2, rw_kk, rw_ka, rw_rk, rw_gn_g, rw_gn_b, mla_qa_g, mla_w_uq, mla_kva_g, mla_w_uk, mla_w_uv, mla_q_norm_g, mla_k_norm_g, sgu_w, sgu_b, sgu_ln_g, sgu_ln_b, pool_w, pool_scale):
    depth = w_in.shape[0]
    t_p = x_prompt.shape[1]
    t_s = x_sample.shape[1]
    past = cache_ckv.shape[2]
    sgu_sizes = sorted({min(SGU_CHUNK, t_p), min(SGU_CHUNK, t_s)})
    tabs_p = _rope_tables(jnp.arange(t_p))
    tabs_s = _rope_tables(past + jnp.arange(t_s))
    yp, ys = x_prompt, x_sample
    new_p = [[] for _ in range(5)]
    new_s = [[] for _ in range(6)]
    for l in range(depth):
        lp = _layer_params(l, norm_g, w_in, w_out, rw_mu, rw_w0, rw_w2, rw_a0, rw_a2, rw_kk, rw_ka, rw_rk, rw_gn_g,
                           rw_gn_b, mla_qa_g, mla_w_uq, mla_kva_g, mla_w_uk, mla_w_uv, mla_q_norm_g, mla_k_norm_g,
                           sgu_w, sgu_b, sgu_ln_g, sgu_ln_b, pool_w, pool_scale, sgu_sizes)
        yp, st_p = _layer(yp, lp, tabs_p, None)
        ys, st_s = _layer(ys, lp, tabs_s, (cache_ckv[l], cache_krope[l], state_wkv[l], state_shift[l], state_pool[l]))
        for lst, arr in zip(new_p, st_p):
            lst.append(arr)
        for lst, arr in zip(new_s, st_s):
            lst.append(arr)
    outs_p = [jnp.stack(a, axis=0) for a in new_p]
    outs_s = [jnp.stack(a, axis=0) for a in new_s]
    return (yp, ys, *outs_p, *outs_s)
```

```python
import functools
import math

import jax
import jax.numpy as jnp
import numpy as np
from jax import lax
from jax.experimental import pallas as pl
from jax.experimental.pallas import tpu as pltpu

F32 = jnp.float32
BF16 = jnp.bfloat16

D_MODEL = 1024
CHUNK = 64
BRANCH = 256
NORM_EPS = 1e-6
HEADS = 4
HD = 64
RW_LORA = 64
RW_SHIFT = 3 * BRANCH + 2 * RW_LORA
RW_GN_EPS = 64e-5
NOPE = 64
ROPE = 32
QK_HD = NOPE + ROPE
Q_RANK = 192
KV_RANK = 128
ROPE_THETA = 10000.0
SGU_CHUNK = 128
SGU_EPS = 1e-5
POOL_WINDOWS = (2, 4, 8, 16)
POOL_HIST = 15
D_IN = 3040

LANE = 128
HALO = 16
Q_PAD = 256
HEAD_PAD = 128
VMEM_LIMIT = 48 * 1024 * 1024

SEG_A = (0, 896)
SEG_B = (896, 1408)
SEG_C = (1408, 1920)
SEG_D = (1920, 2176)
SEG_G = (2176, 3200)
D_IN_PAD = 3200


def _params(sem):
    return pltpu.CompilerParams(dimension_semantics=sem, vmem_limit_bytes=VMEM_LIMIT)


def _full(shape):
    nd = len(shape)
    return pl.BlockSpec(shape, lambda *_: (0,) * nd)


def _inproj_kernel(x_ref, g_ref, w_ref, za_ref, zb_ref, zc_ref, zd_ref, zg_ref):
    x = x_ref[...]
    ms = jnp.mean(x * x, axis=-1, keepdims=True)
    xn = (x * lax.rsqrt(ms + NORM_EPS) * g_ref[...]).astype(BF16)
    for ref, (lo, hi) in ((za_ref, SEG_A), (zb_ref, SEG_B), (zc_ref, SEG_C), (zd_ref, SEG_D), (zg_ref, SEG_G)):
        ref[...] = jnp.dot(xn, w_ref[:, lo:hi], preferred_element_type=F32)


def _inproj(x2d, g, w):
    n = x2d.shape[0]
    tm = min(512, n)
    widths = [hi - lo for lo, hi in (SEG_A, SEG_B, SEG_C, SEG_D, SEG_G)]
    return pl.pallas_call(
        _inproj_kernel,
        grid=(n // tm,),
        in_specs=[pl.BlockSpec((tm, D_MODEL), lambda i: (i, 0)), _full((1, D_MODEL)), _full((D_MODEL, D_IN_PAD))],
        out_specs=[pl.BlockSpec((tm, wd), lambda i: (i, 0)) for wd in widths],
        out_shape=[jax.ShapeDtypeStruct((n, wd), F32) for wd in widths],
        compiler_params=_params(("parallel",)),
        name="inproj",
    )(x2d, g, w)


def _softplus(x):
    return jnp.maximum(x, 0.0) + jnp.log(1.0 + jnp.exp(-jnp.abs(x)))


def _split_bf16(x, parts):
    out = []
    for _ in range(parts - 1):
        hi = x.astype(BF16)
        out.append(hi)
        x = x - hi.astype(F32)
    out.append(x.astype(BF16))
    return out


def _head_sum(x, ones_bd):
    return sum(jnp.dot(p, ones_bd, preferred_element_type=F32) for p in _split_bf16(x, 2))


def _dot_s(a, b):
    return jnp.dot(a.astype(BF16), b.astype(BF16), preferred_element_type=F32)


def _dot_nt_s(a, b):
    return lax.dot_general(a.astype(BF16), b.astype(BF16), (((1,), (1,)), ((), ())), preferred_element_type=F32)


def _rwkv_kernel(za_ref, shift_ref, z0_ref, mu_ref, wl_ref, w0_ref, a0_ref, kkg_ref, kag_ref, rkg_ref,
                 gng_ref, gnb_ref, ya_ref, zfin_ref, z_scr, carry_scr, *, c, levels, rows):
    ci = pl.program_id(1)

    @pl.when(ci == 0)
    def _():
        z_scr[...] = z0_ref[...]
        for i in range(rows):
            carry_scr[i] = jnp.broadcast_to(shift_ref[i], carry_scr.shape[1:])

    zas = [za_ref[i] for i in range(rows)]
    res = _run_lockstep([_rwkv_chunk(zas[i], carry_scr[i, 0:1, :], z_scr[i], mu_ref, wl_ref, w0_ref, a0_ref,
                                     kkg_ref, kag_ref, rkg_ref, gng_ref, gnb_ref, c=c, levels=levels)
                         for i in range(rows)])
    ya_ref[...] = jnp.stack([r[0] for r in res], axis=0)
    z_scr[...] = jnp.stack([r[1] for r in res], axis=0)
    carry_scr[...] = jnp.stack([jnp.broadcast_to(za[c - 1:c, :], carry_scr.shape[1:]) for za in zas], axis=0)

    @pl.when(ci == pl.num_programs(1) - 1)
    def _():
        zfin_ref[...] = z_scr[...]


def _rwkv_chunk(za, carry, z, mu_ref, wl_ref, w0_ref, a0_ref, kkg_ref, kag_ref, rkg_ref, gng_ref, gnb_ref, *,
                c, levels):
    hc = HEADS * c
    row = lax.broadcasted_iota(jnp.int32, za.shape, 0)
    prev = jnp.where(row == 0, carry, pltpu.roll(za, 1, 0))
    zs = za + mu_ref[...] * (prev - za)
    r = zs[:, 0:BRANCH]
    k = zs[:, BRANCH:2 * BRANCH]
    v = zs[:, 2 * BRANCH:3 * BRANCH]
    wa = zs[:, 3 * BRANCH:RW_SHIFT]
    lane128 = lax.broadcasted_iota(jnp.int32, wa.shape, 1)
    lora_in = jnp.where(lane128 < RW_LORA, jnp.tanh(wa), wa).astype(BF16)
    lora = jnp.dot(lora_in, wl_ref[...], preferred_element_type=F32)
    w_log = -_softplus(-(w0_ref[...] + lora[:, :BRANCH])) - 0.5
    logd = -jnp.exp(w_log)
    asig = 1.0 / (1.0 + jnp.exp(-(a0_ref[...] + lora[:, BRANCH:])))

    lane = lax.broadcasted_iota(jnp.int32, (1, BRANCH), 1)
    rr = lax.broadcasted_iota(jnp.int32, (BRANCH, BRANCH), 0)
    cc = lax.broadcasted_iota(jnp.int32, (BRANCH, BRANCH), 1)
    ones_bd = jnp.where(rr // HD == cc // HD, 1.0, 0.0).astype(BF16)

    kkv = k * kkg_ref[...]
    kk = kkv * lax.rsqrt(jnp.maximum(_head_sum(kkv * kkv, ones_bd), 1e-24))
    kmod = k * (1.0 + (asig - 1.0) * kag_ref[...])
    avec = -kk
    bvec = kk * asig

    tr = lax.broadcasted_iota(jnp.int32, (c, c), 0)
    tc = lax.broadcasted_iota(jnp.int32, (c, c), 1)
    tril = jnp.where(tc <= tr, 1.0, 0.0).astype(BF16)
    lcum = sum(jnp.dot(tril, p, preferred_element_type=F32) for p in _split_bf16(logd, 3))
    lexc = lcum - logd
    ltot = lcum[c - 1:c, :]
    g_in = jnp.exp(lcum)
    g_ex = jnp.exp(lexc)
    g_inv = jnp.exp(-lcum)
    g_rem = jnp.exp(ltot - lcum)

    def stack(x):
        return jnp.concatenate([jnp.where(lane // HD == h, x, 0.0) for h in range(HEADS)], axis=0)

    a_s = stack(avec * g_ex)
    r_s = stack(r * g_in)
    b_s = stack(bvec * g_inv)
    k_s = stack(kmod * g_inv)
    bq_s = stack(bvec * g_rem)
    kq_s = stack(kmod * g_rem)
    v_s = stack(v)

    sr = lax.broadcasted_iota(jnp.int32, (hc, hc), 0)
    sc = lax.broadcasted_iota(jnp.int32, (hc, hc), 1)
    same = (sr // c) == (sc // c)
    strict = lambda x: jnp.where(same, jnp.where(sc < sr, x, 0.0), 0.0)
    incl = lambda x: jnp.where(same, jnp.where(sc <= sr, x, 0.0), 0.0)
    yield
    a_ab = strict(_dot_nt_s(a_s, b_s))
    a_ak = strict(_dot_nt_s(a_s, k_s))
    a_rb = incl(_dot_nt_s(r_s, b_s))
    a_rk = incl(_dot_nt_s(r_s, k_s))
    w_z = _dot_s(a_s, z) + _dot_s(a_ak, v_s)
    yield

    eye = jnp.where(sr == sc, 1.0, 0.0).astype(F32)
    t_inv = eye + a_ab
    pw = a_ab
    for _ in range(levels - 1):
        pw = _dot_s(pw, pw)
        yield
        t_inv = t_inv + _dot_s(t_inv, pw)

    yield
    u = _dot_s(t_inv, w_z)
    y_part = _dot_s(r_s, z) + _dot_s(a_rk, v_s)
    g_col = jnp.exp(jnp.sum(logd.T, axis=1, keepdims=True))
    z_part = z * g_col + _dot_s(kq_s.T, v_s)
    yield
    y_s = y_part + _dot_s(a_rb, u)
    z_new = z_part + _dot_s(bq_s.T, u)
    yield

    y = y_s[0:c]
    for h in range(1, HEADS):
        y = y + y_s[h * c:(h + 1) * c]

    mean = _head_sum(y, ones_bd) * (1.0 / HD)
    yield
    yc = y - mean
    var = _head_sum(yc * yc, ones_bd) * (1.0 / HD)
    yn = yc * lax.rsqrt(var + RW_GN_EPS) * gng_ref[...] + gnb_ref[...]
    bonus = _head_sum(r * kmod * rkg_ref[...], ones_bd) * v
    return yn + bonus, z_new


def _run_lockstep(gens):
    results = [None] * len(gens)
    active = list(enumerate(gens))
    while active:
        still = []
        for i, g in active:
            try:
                next(g)
                still.append((i, g))
            except StopIteration as stop:
                results[i] = stop.value
        active = still
    return results


RWKV_ROWS = 4


def _rwkv(za, shift_prev, zbd0, lp):
    b, t, _ = za.shape
    c = min(CHUNK, t)
    levels = int(math.log2(c))
    rows = math.gcd(RWKV_ROWS, b)
    row = lambda w: _full((1, w))
    return pl.pallas_call(
        functools.partial(_rwkv_kernel, c=c, levels=levels, rows=rows),
        grid=(b // rows, t // c),
        in_specs=[pl.BlockSpec((rows, c, RW_SHIFT), lambda i, j: (i, j, 0)),
                  pl.BlockSpec((rows, 1, RW_SHIFT), lambda i, j: (i, 0, 0)),
                  pl.BlockSpec((rows, BRANCH, BRANCH), lambda i, j: (i, 0, 0)),
                  row(RW_SHIFT), _full((2 * RW_LORA, 2 * BRANCH)), row(BRANCH), row(BRANCH), row(BRANCH),
                  row(BRANCH), row(BRANCH), row(BRANCH), row(BRANCH)],
        out_specs=[pl.BlockSpec((rows, c, BRANCH), lambda i, j: (i, j, 0)),
                   pl.BlockSpec((rows, BRANCH, BRANCH), lambda i, j: (i, 0, 0))],
        out_shape=[jax.ShapeDtypeStruct((b, t, BRANCH), F32), jax.ShapeDtypeStruct((b, BRANCH, BRANCH), F32)],
        scratch_shapes=[pltpu.VMEM((rows, BRANCH, BRANCH), F32), pltpu.VMEM((rows, 8, RW_SHIFT), F32)],
        compiler_params=_params(("parallel", "arbitrary")),
        name="rwkv",
    )(za, shift_prev, zbd0, lp["rw_mu"], lp["rw_lora"], lp["rw_w0"], lp["rw_a0"], lp["rw_kk"], lp["rw_ka"],
      lp["rw_rk"], lp["rw_gn_g"], lp["rw_gn_b"])


def _rope_apply(x, cos, s1, s2):
    half = ROPE // 2
    return x * cos + pltpu.roll(x, LANE - half, 1) * s1 + pltpu.roll(x, half, 1) * s2


def _mla_q_kernel(zb_ref, cq_ref, s1q_ref, s2q_ref, ck_ref, s1k_ref, s2k_ref, gqa_ref, wuq_ref, gqn_ref,
                  gkva_ref, gkr_ref, q_ref, ckv_ref, kr_ref):
    z = zb_ref[0]
    zq = z[:, 0:Q_PAD]
    qa = zq * lax.rsqrt(jnp.sum(zq * zq, axis=-1, keepdims=True) * (1.0 / Q_RANK) + NORM_EPS) * gqa_ref[...]
    q = jnp.dot(qa.astype(BF16), wuq_ref[...], preferred_element_type=F32)
    lane = lax.broadcasted_iota(jnp.int32, (1, HEAD_PAD), 1)
    is_nope = lane < NOPE
    sm_scale = 1.0 / math.sqrt(QK_HD)
    for h in range(HEADS):
        qh = q[:, h * HEAD_PAD:(h + 1) * HEAD_PAD]
        sq = qh * qh
        sn = jnp.sum(jnp.where(is_nope, sq, 0.0), axis=-1, keepdims=True) * (1.0 / NOPE)
        sq_rope = jnp.where(is_nope, 0.0, jnp.where(lane < QK_HD, sq, 0.0))
        sr = jnp.sum(sq_rope, axis=-1, keepdims=True) * (1.0 / ROPE)
        scale = jnp.where(is_nope, lax.rsqrt(sn + NORM_EPS), lax.rsqrt(sr + NORM_EPS))
        xr = qh * scale * gqn_ref[...]
        out = _rope_apply(xr, cq_ref[...], s1q_ref[...], s2q_ref[...])
        q_ref[0, :, h * HEAD_PAD:(h + 1) * HEAD_PAD] = (out * sm_scale).astype(BF16)
    zkv = z[:, Q_PAD:Q_PAD + KV_RANK]
    ckv_ref[0] = zkv * lax.rsqrt(jnp.mean(zkv * zkv, axis=-1, keepdims=True) + NORM_EPS) * gkva_ref[...]
    zkr = z[:, Q_PAD + KV_RANK:]
    xr = zkr * lax.rsqrt(jnp.sum(zkr * zkr, axis=-1, keepdims=True) * (1.0 / ROPE) + NORM_EPS) * gkr_ref[...]
    kr_ref[0] = _rope_apply(xr, ck_ref[...], s1k_ref[...], s2k_ref[...])


def _mla_q(zb, tabs, lp):
    b, t, _ = zb.shape
    tm = min(512, t)
    tab = pl.BlockSpec((tm, LANE), lambda i, j: (j, 0))
    row = lambda w: _full((1, w))
    return pl.pallas_call(
        _mla_q_kernel,
        grid=(b, t // tm),
        in_specs=[pl.BlockSpec((1, tm, 512), lambda i, j: (i, j, 0)), tab, tab, tab, tab, tab, tab,
                  row(Q_PAD), _full((Q_PAD, HEADS * HEAD_PAD)), row(HEAD_PAD), row(KV_RANK), row(LANE)],
        out_specs=[pl.BlockSpec((1, tm, HEADS * HEAD_PAD), lambda i, j: (i, j, 0)),
                   pl.BlockSpec((1, tm, KV_RANK), lambda i, j: (i, j, 0)),
                   pl.BlockSpec((1, tm, LANE), lambda i, j: (i, j, 0))],
        out_shape=[jax.ShapeDtypeStruct((b, t, HEADS * HEAD_PAD), BF16),
                   jax.ShapeDtypeStruct((b, t, KV_RANK), F32),
                   jax.ShapeDtypeStruct((b, t, LANE), F32)],
        compiler_params=_params(("parallel", "parallel")),
        name="mla_q",
    )(zb, *tabs, lp["mla_qa_g"], lp["mla_w_uq"], lp["mla_qn_g"], lp["mla_kva_g"], lp["mla_kr_g"])


def _mla_kv_kernel(ckv_ref, kr_ref, wuk_ref, gkn_ref, wuv_ref, k_ref, v_ref):
    cb = ckv_ref[0].astype(BF16)
    kn = jnp.dot(cb, wuk_ref[...], preferred_element_type=F32)
    krs = pltpu.roll(kr_ref[0], NOPE, 1)
    for h in range(HEADS):
        kh = kn[:, h * HEAD_PAD:(h + 1) * HEAD_PAD]
        ms = jnp.sum(kh * kh, axis=-1, keepdims=True) * (1.0 / NOPE)
        k_ref[0, :, h * HEAD_PAD:(h + 1) * HEAD_PAD] = (kh * lax.rsqrt(ms + NORM_EPS) * gkn_ref[...] + krs).astype(BF16)
    v_ref[0] = jnp.dot(cb, wuv_ref[...], preferred_element_type=F32).astype(BF16)


def _mla_kv(ckv, kr, lp):
    b, n, _ = ckv.shape
    tm = 512 if n % 512 == 0 else n
    wide = HEADS * HEAD_PAD
    return pl.pallas_call(
        _mla_kv_kernel,
        grid=(b, n // tm),
        in_specs=[pl.BlockSpec((1, tm, KV_RANK), lambda i, j: (i, j, 0)),
                  pl.BlockSpec((1, tm, LANE), lambda i, j: (i, j, 0)),
                  _full((KV_RANK, wide)), _full((1, HEAD_PAD)), _full((KV_RANK, wide))],
        out_specs=[pl.BlockSpec((1, tm, wide), lambda i, j: (i, j, 0)),
                   pl.BlockSpec((1, tm, wide), lambda i, j: (i, j, 0))],
        out_shape=[jax.ShapeDtypeStruct((b, n, wide), BF16), jax.ShapeDtypeStruct((b, n, wide), BF16)],
        compiler_params=_params(("parallel", "parallel")),
        name="mla_kv",
    )(ckv, kr, lp["mla_w_uk"], lp["mla_kn_g"], lp["mla_w_uv"])


def _flash_kernel(qt_ref, kt_ref, q_ref, k_ref, v_ref, o_ref, m_scr, l_scr, acc_scr, *, tq, tk, causal, kv_valid):
    p_id = pl.program_id(1)
    qi = qt_ref[p_id]
    ki = kt_ref[p_id]

    @pl.when(ki == 0)
    def _():
        m_scr[...] = jnp.full(m_scr.shape, -1e30, F32)
        l_scr[...] = jnp.zeros(l_scr.shape, F32)
        acc_scr[...] = jnp.zeros(acc_scr.shape, F32)

    def update(mask):
        for h in range(HEADS):
            sl = slice(h * HEAD_PAD, (h + 1) * HEAD_PAD)
            s = lax.dot_general(q_ref[0, :, sl], k_ref[0, :, sl], (((1,), (1,)), ((), ())),
                                preferred_element_type=F32)
            if mask is not None:
                s = jnp.where(mask, s, -1e30)
            m_prev = m_scr[h]
            m_new = jnp.maximum(m_prev, jnp.max(s, axis=-1, keepdims=True))
            alpha = jnp.exp(m_prev - m_new)
            p = jnp.exp(s - pltpu.repeat(m_new, tk // LANE, 1))
            l_scr[h] = alpha * l_scr[h] + jnp.sum(p, axis=-1, keepdims=True)
            m_scr[h] = m_new
            acc_scr[h] = alpha * acc_scr[h] + jnp.dot(p.astype(BF16), v_ref[0, :, sl], preferred_element_type=F32)

    if causal:
        @pl.when(ki < qi)
        def _():
            update(None)

        @pl.when(ki == qi)
        def _():
            r = lax.broadcasted_iota(jnp.int32, (tq, tk), 0)
            c = lax.broadcasted_iota(jnp.int32, (tq, tk), 1)
            update((r // CHUNK) >= (c // CHUNK))
        is_last = ki == qi
    else:
        c = lax.broadcasted_iota(jnp.int32, (tq, tk), 1)
        update(c < kv_valid)
        is_last = ki == 0

    @pl.when(is_last)
    def _():
        outs = [acc_scr[h] / l_scr[h] for h in range(HEADS)]
        o_ref[0, :, 0:LANE] = outs[0] + pltpu.roll(outs[1], HD, 1)
        o_ref[0, :, LANE:2 * LANE] = outs[2] + pltpu.roll(outs[3], HD, 1)


def _flash(q, k, v, *, causal, kv_valid):
    b, t, wide = q.shape
    n = k.shape[1]
    if causal:
        tq = tk = min(512, t)
        pairs = [(i, j) for i in range(t // tq) for j in range(i + 1)]
    else:
        tq, tk = t, n
        pairs = [(0, 0)]
    qt = jnp.asarray(np.array([p[0] for p in pairs], np.int32))
    kt = jnp.asarray(np.array([p[1] for p in pairs], np.int32))
    grid_spec = pltpu.PrefetchScalarGridSpec(
        num_scalar_prefetch=2,
        grid=(b, len(pairs)),
        in_specs=[pl.BlockSpec((1, tq, wide), lambda i, p, qt, kt: (i, qt[p], 0)),
                  pl.BlockSpec((1, tk, wide), lambda i, p, qt, kt: (i, kt[p], 0)),
                  pl.BlockSpec((1, tk, wide), lambda i, p, qt, kt: (i, kt[p], 0))],
        out_specs=pl.BlockSpec((1, tq, BRANCH), lambda i, p, qt, kt: (i, qt[p], 0)),
        scratch_shapes=[pltpu.VMEM((HEADS, tq, LANE), F32), pltpu.VMEM((HEADS, tq, LANE), F32),
                        pltpu.VMEM((HEADS, tq, HEAD_PAD), F32)],
    )
    return pl.pallas_call(
        functools.partial(_flash_kernel, tq=tq, tk=tk, causal=causal, kv_valid=kv_valid),
        grid_spec=grid_spec,
        out_shape=jax.ShapeDtypeStruct((b, t, BRANCH), F32),
        compiler_params=_params(("parallel", "arbitrary")),
        name="flash",
    )(qt, kt, q, k, v)


def _combine_kernel(x_ref, ya_ref, yb_ref, zc_ref, zd_ref, halo_ref, zg_ref, lng_ref, lnb_ref, ws_ref, sb_ref,
                    pw_ref, psc_ref, wout_ref, *out_refs, tm, cs, n_hist, halo_is_prev_tile):
    out_ref = out_refs[0]
    ti = pl.program_id(1)
    lane = lax.broadcasted_iota(jnp.int32, (1, BRANCH), 1)

    zc = zc_ref[0]
    u = zc[:, 0:BRANCH]
    vraw = zc[:, BRANCH:]
    xc = vraw - jnp.mean(vraw, axis=-1, keepdims=True)
    var = jnp.mean(xc * xc, axis=-1, keepdims=True)
    vn = xc * lax.rsqrt(var + SGU_EPS) * lng_ref[...] + lnb_ref[...]
    if len(out_refs) > 1:
        out_refs[1][0] = vn
    vnb = vn.astype(BF16)
    zero = jnp.zeros((), BF16)
    parts = []
    for n in range(tm // cs):
        vch = vnb[n * cs:(n + 1) * cs]
        stacked = jnp.concatenate([jnp.where(lane // HD == h, vch, zero) for h in range(HEADS)], axis=0)
        parts.append(jnp.dot(ws_ref[...], stacked, preferred_element_type=F32) + sb_ref[...])
    yc = u * (parts[0] if len(parts) == 1 else jnp.concatenate(parts, axis=0))

    zd = zd_ref[0]
    halo = halo_ref[0]
    if halo_is_prev_tile:
        halo = jnp.where(ti == 0, 0.0, halo)
    ext = jnp.concatenate([halo, zd], axis=0)
    sums = []
    acc = ext
    for sh in (1, 2, 4, 8):
        acc = acc + pltpu.roll(acc, sh, 0)
        sums.append(acc[HALO:])
    t_idx = lax.broadcasted_iota(jnp.int32, (tm, BRANCH), 0) + (ti * tm + n_hist + 1)
    lane2 = lax.broadcasted_iota(jnp.int32, (tm, BRANCH), 1)
    win = jnp.where(lane2 < HD, POOL_WINDOWS[0],
                    jnp.where(lane2 < 2 * HD, POOL_WINDOWS[1],
                              jnp.where(lane2 < 3 * HD, POOL_WINDOWS[2], POOL_WINDOWS[3])))
    cnt = jnp.minimum(t_idx, win).astype(F32)
    ssum = jnp.where(lane2 < HD, sums[0],
                     jnp.where(lane2 < 2 * HD, sums[1], jnp.where(lane2 < 3 * HD, sums[2], sums[3])))
    d = ssum / cnt - zd
    yd = jnp.dot(d.astype(BF16), pw_ref[...], preferred_element_type=F32) * psc_ref[...]

    g = zg_ref[0]
    gate = g / (1.0 + jnp.exp(-g))
    y = jnp.concatenate([ya_ref[0], yb_ref[0], yc, yd], axis=-1) * gate
    out_ref[0] = x_ref[0] + jnp.dot(y.astype(BF16), wout_ref[...], preferred_element_type=F32)


def _combine(x, ya, yb, zc, zd, halo, zg, lp, *, n_hist, want_vn):
    b, t, _ = x.shape
    tm = min(512, t)
    cs = min(SGU_CHUNK, t)
    halo_is_prev_tile = halo is None
    if halo_is_prev_tile:
        halo = zd
        per = tm // HALO
        halo_spec = pl.BlockSpec((1, HALO, BRANCH), lambda i, j: (i, jnp.maximum(j * per - 1, 0), 0))
    else:
        halo_spec = pl.BlockSpec((1, HALO, BRANCH), lambda i, j: (i, 0, 0))
    blk = lambda w: pl.BlockSpec((1, tm, w), lambda i, j: (i, j, 0))
    row = lambda w: _full((1, w))
    out_specs = [blk(D_MODEL)]
    out_shape = [jax.ShapeDtypeStruct((b, t, D_MODEL), F32)]
    if want_vn:
        out_specs.append(blk(BRANCH))
        out_shape.append(jax.ShapeDtypeStruct((b, t, BRANCH), F32))
    res = pl.pallas_call(
        functools.partial(_combine_kernel, tm=tm, cs=cs, n_hist=n_hist, halo_is_prev_tile=halo_is_prev_tile),
        grid=(b, t // tm),
        in_specs=[blk(D_MODEL), blk(BRANCH), blk(BRANCH), blk(2 * BRANCH), blk(BRANCH), halo_spec, blk(D_MODEL),
                  row(BRANCH), row(BRANCH), _full((cs, HEADS * cs)), _full((cs, BRANCH)),
                  _full((BRANCH, BRANCH)), row(BRANCH), _full((D_MODEL, D_MODEL))],
        out_specs=out_specs,
        out_shape=out_shape,
        compiler_params=_params(("parallel", "parallel")),
        name="combine",
    )(x, ya, yb, zc, zd, halo, zg, lp["sgu_ln_g"], lp["sgu_ln_b"], lp["sgu_w_cat"][cs], lp["sgu_b_tab"][cs],
      lp["pool_w_bd"], lp["pool_scale"], lp["w_out"])
    return res


def _pad_cols(w, total):
    return jnp.pad(w, ((0, 0), (0, total - w.shape[1])))


def _head_slots(w, real):
    rows = w.shape[0]
    w = w.reshape(rows, HEADS, real)
    return jnp.pad(w, ((0, 0), (0, 0), (0, HEAD_PAD - real))).reshape(rows, HEADS * HEAD_PAD)


def _layer_params(l, norm_g, w_in, w_out, rw_mu, rw_w0, rw_w2, rw_a0, rw_a2, rw_kk, rw_ka, rw_rk, rw_gn_g,
                  rw_gn_b, mla_qa_g, mla_w_uq, mla_kva_g, mla_w_uk, mla_w_uv, mla_q_norm_g, mla_k_norm_g, sgu_w,
                  sgu_b, sgu_ln_g, sgu_ln_b, pool_w, pool_scale, sgu_sizes):
    wi = w_in[l]
    off_b = RW_SHIFT
    off_c = off_b + Q_RANK + KV_RANK + ROPE
    off_d = off_c + 2 * BRANCH
    off_g = off_d + BRANCH
    w_in_r = jnp.concatenate([
        wi[:, :off_b],
        _pad_cols(wi[:, off_b:off_b + Q_RANK], Q_PAD),
        wi[:, off_b + Q_RANK:off_b + Q_RANK + KV_RANK],
        _pad_cols(wi[:, off_b + Q_RANK + KV_RANK:off_c], LANE),
        wi[:, off_c:off_d], wi[:, off_d:off_g], wi[:, off_g:]], axis=1).astype(BF16)
    zeros = jnp.zeros((RW_LORA, BRANCH), F32)
    rw_lora = jnp.concatenate([jnp.concatenate([rw_w2[l], zeros], 1), jnp.concatenate([zeros, rw_a2[l]], 1)], 0)
    row = lambda v: v.reshape(1, -1).astype(F32)
    qn = mla_q_norm_g[l]
    kn = mla_k_norm_g[l]
    tri = jnp.tril(jnp.ones((SGU_CHUNK, SGU_CHUNK), bool))
    ws = jnp.where(tri[None], sgu_w[l], 0.0)
    sgu_w_cat = {cs: jnp.concatenate([ws[h, :cs, :cs] for h in range(HEADS)], axis=1).astype(BF16)
                 for cs in sgu_sizes}
    sgu_b_tab = {cs: jnp.repeat(jnp.transpose(sgu_b[l])[:cs], HD, axis=1) for cs in sgu_sizes}
    eye = jnp.eye(HEADS, dtype=F32)
    pool_w_bd = jnp.einsum("gcd,gh->gchd", pool_w[l], eye).reshape(BRANCH, BRANCH).astype(BF16)
    return {
        "norm_g": row(norm_g[l]), "w_in": w_in_r, "w_out": w_out[l].astype(BF16),
        "rw_mu": row(rw_mu[l]), "rw_lora": rw_lora.astype(BF16), "rw_w0": row(rw_w0[l]), "rw_a0": row(rw_a0[l]),
        "rw_kk": row(rw_kk[l]), "rw_ka": row(rw_ka[l]), "rw_rk": row(rw_rk[l]), "rw_gn_g": row(rw_gn_g[l]),
        "rw_gn_b": row(rw_gn_b[l]),
        "mla_qa_g": _pad_cols(row(mla_qa_g[l]), Q_PAD),
        "mla_w_uq": jnp.pad(_head_slots(mla_w_uq[l], QK_HD), ((0, Q_PAD - Q_RANK), (0, 0))).astype(BF16),
        "mla_qn_g": _pad_cols(row(qn), HEAD_PAD),
        "mla_kva_g": row(mla_kva_g[l]),
        "mla_kr_g": _pad_cols(row(kn[NOPE:]), LANE),
        "mla_w_uk": _head_slots(mla_w_uk[l], NOPE).astype(BF16),
        "mla_kn_g": _pad_cols(row(kn[:NOPE]), HEAD_PAD),
        "mla_w_uv": _head_slots(mla_w_uv[l], HD).astype(BF16),
        "sgu_ln_g": row(sgu_ln_g[l]), "sgu_ln_b": row(sgu_ln_b[l]), "sgu_w_cat": sgu_w_cat, "sgu_b_tab": sgu_b_tab,
        "pool_w_bd": pool_w_bd, "pool_scale": row(pool_scale[l]),
    }


def _rope_tables(pos):
    half = ROPE // 2
    inv = ROPE_THETA ** (-jnp.arange(half, dtype=F32) / half)
    ang = pos.astype(F32)[:, None] * inv[None, :]
    cos, sin = jnp.cos(ang), jnp.sin(ang)
    t = pos.shape[0]

    def place(first, second, off, fill):
        base = jnp.full((t, LANE), fill, F32)
        base = lax.dynamic_update_slice(base, first, (0, off))
        return lax.dynamic_update_slice(base, second, (0, off + half))

    z = jnp.zeros_like(sin)
    tabs = []
    for off, fill in ((NOPE, 1.0), (0, 0.0)):
        tabs += [place(cos, cos, off, fill), place(-sin, z, off, 0.0), place(z, sin, off, 0.0)]
    return tabs


def _state_to_blockdiag(s):
    b = s.shape[0]
    eye = jnp.eye(HEADS, dtype=s.dtype)
    return jnp.einsum("bhij,hg->bhjgi", s, eye).reshape(b, BRANCH, BRANCH)


def _blockdiag_to_state(z):
    b = z.shape[0]
    z5 = z.reshape(b, HEADS, HD, HEADS, HD)
    return jnp.stack([jnp.swapaxes(z5[:, h, :, h, :], 1, 2) for h in range(HEADS)], axis=1)


def _layer(x, lp, tabs, hist):
    b, t, _ = x.shape
    za, zb, zc, zd, zg = _inproj(x.reshape(b * t, D_MODEL), lp["norm_g"], lp["w_in"])
    za = za.reshape(b, t, RW_SHIFT)
    zb = zb.reshape(b, t, 512)
    zc = zc.reshape(b, t, 2 * BRANCH)
    zd = zd.reshape(b, t, BRANCH)
    zg = zg.reshape(b, t, D_MODEL)
    if hist is None:
        shift_prev = jnp.zeros((b, 1, RW_SHIFT), F32)
        zbd0 = jnp.zeros((b, BRANCH, BRANCH), F32)
    else:
        c_ckv, c_kr, s_wkv, s_shift, s_pool = hist
        shift_prev = s_shift[:, None, :]
        zbd0 = _state_to_blockdiag(s_wkv)
    ya, zfin = _rwkv(za, shift_prev, zbd0, lp)
    q, ckv, krp = _mla_q(zb, tabs, lp)
    if hist is None:
        k, v = _mla_kv(ckv, krp, lp)
        yb = _flash(q, k, v, causal=True, kv_valid=None)
        halo, n_hist = None, 0
        pool_src = zd
    else:
        past = c_ckv.shape[1]
        n_all = past + t
        n_pad = -(-n_all // LANE) * LANE
        ckv_all = jnp.pad(jnp.concatenate([c_ckv, ckv], axis=1), ((0, 0), (0, n_pad - n_all), (0, 0)))
        kr_all = jnp.concatenate([jnp.pad(c_kr, ((0, 0), (0, 0), (0, LANE - ROPE))), krp], axis=1)
        kr_all = jnp.pad(kr_all, ((0, 0), (0, n_pad - n_all), (0, 0)))
        k, v = _mla_kv(ckv_all, kr_all, lp)
        yb = _flash(q, k, v, causal=False, kv_valid=n_all)
        halo = jnp.pad(s_pool, ((0, 0), (HALO - POOL_HIST, 0), (0, 0)))
        n_hist = POOL_HIST
        pool_src = jnp.concatenate([s_pool, zd], axis=1)
    res = _combine(x, ya, yb, zc, zd, halo, zg, lp, n_hist=n_hist, want_vn=hist is not None)
    state = (ckv, krp[..., :ROPE], _blockdiag_to_state(zfin), za[:, -1], pool_src[:, pool_src.shape[1] - POOL_HIST:])
    if hist is None:
        return res[0], state
    return res[0], state + (res[1],)


def kernel(x_prompt, x_sample, cache_ckv, cache_krope, state_wkv, state_shift, state_pool, norm_g, w_in, w_out, rw_mu, rw_w0, rw_w2, rw_a0, rw_a2, rw_kk, rw_ka, rw_rk, rw_gn_g, rw_gn_b, mla_qa_g, mla_w_uq, mla_kva_g, mla_w_uk, mla_w_uv, mla_q_norm_g, mla_k_norm_g, sgu_w, sgu_b, sgu_ln_g, sgu_ln_b, pool_w, pool_scale):
    depth = w_in.shape[0]
    t_p = x_prompt.shape[1]
    t_s = x_sample.shape[1]
    past = cache_ckv.shape[2]
    sgu_sizes = sorted({min(SGU_CHUNK, t_p), min(SGU_CHUNK, t_s)})
    tabs_p = _rope_tables(jnp.arange(t_p))
    tabs_s = _rope_tables(past + jnp.arange(t_s))
    yp, ys = x_prompt, x_sample
    new_p = [[] for _ in range(5)]
    new_s = [[] for _ in range(6)]
    for l in range(depth):
        lp = _layer_params(l, norm_g, w_in, w_out, rw_mu, rw_w0, rw_w2, rw_a0, rw_a2, rw_kk, rw_ka, rw_rk, rw_gn_g,
                           rw_gn_b, mla_qa_g, mla_w_uq, mla_kva_g, mla_w_uk, mla_w_uv, mla_q_norm_g, mla_k_norm_g,
                           sgu_w, sgu_b, sgu_ln_g, sgu_ln_b, pool_w, pool_scale, sgu_sizes)
        yp, st_p = _layer(yp, lp, tabs_p, None)
        ys, st_s = _layer(ys, lp, tabs_s, (cache_ckv[l], cache_krope[l], state_wkv[l], state_shift[l], state_pool[l]))
        for lst, arr in zip(new_p, st_p):
            lst.append(arr)
        for lst, arr in zip(new_s, st_s):
            lst.append(arr)
    outs_p = [jnp.stack(a, axis=0) for a in new_p]
    outs_s = [jnp.stack(a, axis=0) for a in new_s]
    return (yp, ys, *outs_p, *outs_s)
```

```python
import functools
import math

import jax
import jax.numpy as jnp
import numpy as np
from jax import lax
from jax.experimental import pallas as pl
from jax.experimental.pallas import tpu as pltpu

F32 = jnp.float32
BF16 = jnp.bfloat16

D_MODEL = 1024
CHUNK = 64
BRANCH = 256
NORM_EPS = 1e-6
HEADS = 4
HD = 64
RW_LORA = 64
RW_SHIFT = 3 * BRANCH + 2 * RW_LORA
RW_GN_EPS = 64e-5
NOPE = 64
ROPE = 32
QK_HD = NOPE + ROPE
Q_RANK = 192
KV_RANK = 128
ROPE_THETA = 10000.0
SGU_CHUNK = 128
SGU_EPS = 1e-5
POOL_WINDOWS = (2, 4, 8, 16)
POOL_HIST = 15
D_IN = 3040

LANE = 128
HALO = 16
Q_PAD = 256
HEAD_PAD = 128
VMEM_LIMIT = 48 * 1024 * 1024

SEG_A = (0, 896)
SEG_B = (896, 1408)
SEG_C = (1408, 1920)
SEG_D = (1920, 2176)
SEG_G = (2176, 3200)
D_IN_PAD = 3200


def _params(sem):
    return pltpu.CompilerParams(dimension_semantics=sem, vmem_limit_bytes=VMEM_LIMIT)


def _full(shape):
    nd = len(shape)
    return pl.BlockSpec(shape, lambda *_: (0,) * nd)


def _inproj_kernel(x_ref, g_ref, w_ref, za_ref, zb_ref, zc_ref, zd_ref, zg_ref):
    x = x_ref[...]
    ms = jnp.mean(x * x, axis=-1, keepdims=True)
    xn = (x * lax.rsqrt(ms + NORM_EPS) * g_ref[...]).astype(BF16)
    for ref, (lo, hi) in ((za_ref, SEG_A), (zb_ref, SEG_B), (zc_ref, SEG_C), (zd_ref, SEG_D), (zg_ref, SEG_G)):
        ref[...] = jnp.dot(xn, w_ref[:, lo:hi], preferred_element_type=F32)


def _inproj(x2d, g, w):
    n = x2d.shape[0]
    tm = min(512, n)
    widths = [hi - lo for lo, hi in (SEG_A, SEG_B, SEG_C, SEG_D, SEG_G)]
    return pl.pallas_call(
        _inproj_kernel,
        grid=(n // tm,),
        in_specs=[pl.BlockSpec((tm, D_MODEL), lambda i: (i, 0)), _full((1, D_MODEL)), _full((D_MODEL, D_IN_PAD))],
        out_specs=[pl.BlockSpec((tm, wd), lambda i: (i, 0)) for wd in widths],
        out_shape=[jax.ShapeDtypeStruct((n, wd), F32) for wd in widths],
        compiler_params=_params(("parallel",)),
        name="inproj",
    )(x2d, g, w)


def _softplus(x):
    return jnp.maximum(x, 0.0) + jnp.log(1.0 + jnp.exp(-jnp.abs(x)))


def _split_bf16(x, parts):
    out = []
    for _ in range(parts - 1):
        hi = x.astype(BF16)
        out.append(hi)
        x = x - hi.astype(F32)
    out.append(x.astype(BF16))
    return out


def _head_sum(x, ones_bd):
    return sum(jnp.dot(p, ones_bd, preferred_element_type=F32) for p in _split_bf16(x, 2))


def _dot_s(a, b):
    return jnp.dot(a.astype(BF16), b.astype(BF16), preferred_element_type=F32)


def _dot_nt_s(a, b):
    return lax.dot_general(a.astype(BF16), b.astype(BF16), (((1,), (1,)), ((), ())), preferred_element_type=F32)


def _rwkv_kernel(za_ref, shift_ref, z0_ref, mu_ref, wl_ref, w0_ref, a0_ref, kkg_ref, kag_ref, rkg_ref,
                 gng_ref, gnb_ref, ya_ref, zfin_ref, z_scr, carry_scr, *, c, levels, rows):
    ci = pl.program_id(1)

    @pl.when(ci == 0)
    def _():
        z_scr[...] = z0_ref[...]
        for i in range(rows):
            carry_scr[i] = jnp.broadcast_to(shift_ref[i], carry_scr.shape[1:])

    zas = [za_ref[i] for i in range(rows)]
    res = _run_lockstep([_rwkv_chunk(zas[i], carry_scr[i, 0:1, :], z_scr[i], mu_ref, wl_ref, w0_ref, a0_ref,
                                     kkg_ref, kag_ref, rkg_ref, gng_ref, gnb_ref, c=c, levels=levels)
                         for i in range(rows)])
    ya_ref[...] = jnp.stack([r[0] for r in res], axis=0)
    z_scr[...] = jnp.stack([r[1] for r in res], axis=0)
    carry_scr[...] = jnp.stack([jnp.broadcast_to(za[c - 1:c, :], carry_scr.shape[1:]) for za in zas], axis=0)

    @pl.when(ci == pl.num_programs(1) - 1)
    def _():
        zfin_ref[...] = z_scr[...]


def _rwkv_chunk(za, carry, z, mu_ref, wl_ref, w0_ref, a0_ref, kkg_ref, kag_ref, rkg_ref, gng_ref, gnb_ref, *,
                c, levels):
    hc = HEADS * c
    row = lax.broadcasted_iota(jnp.int32, za.shape, 0)
    prev = jnp.where(row == 0, carry, pltpu.roll(za, 1, 0))
    zs = za + mu_ref[...] * (prev - za)
    r = zs[:, 0:BRANCH]
    k = zs[:, BRANCH:2 * BRANCH]
    v = zs[:, 2 * BRANCH:3 * BRANCH]
    wa = zs[:, 3 * BRANCH:RW_SHIFT]
    lane128 = lax.broadcasted_iota(jnp.int32, wa.shape, 1)
    lora_in = jnp.where(lane128 < RW_LORA, jnp.tanh(wa), wa).astype(BF16)
    lora = jnp.dot(lora_in, wl_ref[...], preferred_element_type=F32)
    w_log = -_softplus(-(w0_ref[...] + lora[:, :BRANCH])) - 0.5
    logd = -jnp.exp(w_log)
    asig = 1.0 / (1.0 + jnp.exp(-(a0_ref[...] + lora[:, BRANCH:])))

    lane = lax.broadcasted_iota(jnp.int32, (1, BRANCH), 1)
    rr = lax.broadcasted_iota(jnp.int32, (BRANCH, BRANCH), 0)
    cc = lax.broadcasted_iota(jnp.int32, (BRANCH, BRANCH), 1)
    ones_bd = jnp.where(rr // HD == cc // HD, 1.0, 0.0).astype(BF16)

    kkv = k * kkg_ref[...]
    kk = kkv * lax.rsqrt(jnp.maximum(_head_sum(kkv * kkv, ones_bd), 1e-24))
    kmod = k * (1.0 + (asig - 1.0) * kag_ref[...])
    avec = -kk
    bvec = kk * asig

    tr = lax.broadcasted_iota(jnp.int32, (c, c), 0)
    tc = lax.broadcasted_iota(jnp.int32, (c, c), 1)
    tril = jnp.where(tc <= tr, 1.0, 0.0).astype(BF16)
    lcum = sum(jnp.dot(tril, p, preferred_element_type=F32) for p in _split_bf16(logd, 3))
    lexc = lcum - logd
    ltot = lcum[c - 1:c, :]
    g_in = jnp.exp(lcum)
    g_ex = jnp.exp(lexc)
    g_inv = jnp.exp(-lcum)
    g_rem = jnp.exp(ltot - lcum)

    def stack(x):
        return jnp.concatenate([jnp.where(lane // HD == h, x, 0.0) for h in range(HEADS)], axis=0)

    a_s = stack(avec * g_ex)
    r_s = stack(r * g_in)
    b_s = stack(bvec * g_inv)
    k_s = stack(kmod * g_inv)
    bq_s = stack(bvec * g_rem)
    kq_s = stack(kmod * g_rem)
    v_s = stack(v)

    sr = lax.broadcasted_iota(jnp.int32, (hc, hc), 0)
    sc = lax.broadcasted_iota(jnp.int32, (hc, hc), 1)
    same = (sr // c) == (sc // c)
    strict = lambda x: jnp.where(same, jnp.where(sc < sr, x, 0.0), 0.0)
    incl = lambda x: jnp.where(same, jnp.where(sc <= sr, x, 0.0), 0.0)
    yield
    a_ab = strict(_dot_nt_s(a_s, b_s))
    a_ak = strict(_dot_nt_s(a_s, k_s))
    a_rb = incl(_dot_nt_s(r_s, b_s))
    a_rk = incl(_dot_nt_s(r_s, k_s))
    w_z = _dot_s(a_s, z) + _dot_s(a_ak, v_s)
    yield

    eye = jnp.where(sr == sc, 1.0, 0.0).astype(F32)
    t_inv = eye + a_ab
    pw = a_ab
    for _ in range(levels - 1):
        pw = _dot_s(pw, pw)
        yield
        t_inv = t_inv + _dot_s(t_inv, pw)

    yield
    u = _dot_s(t_inv, w_z)
    y_part = _dot_s(r_s, z) + _dot_s(a_rk, v_s)
    g_col = jnp.exp(jnp.sum(logd.T, axis=1, keepdims=True))
    z_part = z * g_col + _dot_s(kq_s.T, v_s)
    yield
    y_s = y_part + _dot_s(a_rb, u)
    z_new = z_part + _dot_s(bq_s.T, u)
    yield

    y = y_s[0:c]
    for h in range(1, HEADS):
        y = y + y_s[h * c:(h + 1) * c]

    mean = _head_sum(y, ones_bd) * (1.0 / HD)
    yield
    yc = y - mean
    var = _head_sum(yc * yc, ones_bd) * (1.0 / HD)
    yn = yc * lax.rsqrt(var + RW_GN_EPS) * gng_ref[...] + gnb_ref[...]
    bonus = _head_sum(r * kmod * rkg_ref[...], ones_bd) * v
    return yn + bonus, z_new


def _run_lockstep(gens):
    results = [None] * len(gens)
    active = list(enumerate(gens))
    while active:
        still = []
        for i, g in active:
            try:
                next(g)
                still.append((i, g))
            except StopIteration as stop:
                results[i] = stop.value
        active = still
    return results


RWKV_ROWS = 4


def _rwkv(za, shift_prev, zbd0, lp):
    b, t, _ = za.shape
    c = min(CHUNK, t)
    levels = int(math.log2(c))
    rows = math.gcd(RWKV_ROWS, b)
    row = lambda w: _full((1, w))
    return pl.pallas_call(
        functools.partial(_rwkv_kernel, c=c, levels=levels, rows=rows),
        grid=(b // rows, t // c),
        in_specs=[pl.BlockSpec((rows, c, RW_SHIFT), lambda i, j: (i, j, 0)),
                  pl.BlockSpec((rows, 1, RW_SHIFT), lambda i, j: (i, 0, 0)),
                  pl.BlockSpec((rows, BRANCH, BRANCH), lambda i, j: (i, 0, 0)),
                  row(RW_SHIFT), _full((2 * RW_LORA, 2 * BRANCH)), row(BRANCH), row(BRANCH), row(BRANCH),
                  row(BRANCH), row(BRANCH), row(BRANCH), row(BRANCH)],
        out_specs=[pl.BlockSpec((rows, c, BRANCH), lambda i, j: (i, j, 0)),
                   pl.BlockSpec((rows, BRANCH, BRANCH), lambda i, j: (i, 0, 0))],
        out_shape=[jax.ShapeDtypeStruct((b, t, BRANCH), F32), jax.ShapeDtypeStruct((b, BRANCH, BRANCH), F32)],
        scratch_shapes=[pltpu.VMEM((rows, BRANCH, BRANCH), F32), pltpu.VMEM((rows, 8, RW_SHIFT), F32)],
        compiler_params=_params(("parallel", "arbitrary")),
        name="rwkv",
    )(za, shift_prev, zbd0, lp["rw_mu"], lp["rw_lora"], lp["rw_w0"], lp["rw_a0"], lp["rw_kk"], lp["rw_ka"],
      lp["rw_rk"], lp["rw_gn_g"], lp["rw_gn_b"])


def _rope_apply(x, cos, s1, s2):
    half = ROPE // 2
    return x * cos + pltpu.roll(x, LANE - half, 1) * s1 + pltpu.roll(x, half, 1) * s2


def _mla_q_kernel(zb_ref, cq_ref, s1q_ref, s2q_ref, ck_ref, s1k_ref, s2k_ref, gqa_ref, wuq_ref, gqn_ref,
                  gkva_ref, gkr_ref, q_ref, ckv_ref, kr_ref):
    z = zb_ref[0]
    zq = z[:, 0:Q_PAD]
    qa = zq * lax.rsqrt(jnp.sum(zq * zq, axis=-1, keepdims=True) * (1.0 / Q_RANK) + NORM_EPS) * gqa_ref[...]
    q = jnp.dot(qa.astype(BF16), wuq_ref[...], preferred_element_type=F32)
    lane = lax.broadcasted_iota(jnp.int32, (1, HEAD_PAD), 1)
    is_nope = lane < NOPE
    sm_scale = math.log2(math.e) / math.sqrt(QK_HD)
    for h in range(HEADS):
        qh = q[:, h * HEAD_PAD:(h + 1) * HEAD_PAD]
        sq = qh * qh
        sn = jnp.sum(jnp.where(is_nope, sq, 0.0), axis=-1, keepdims=True) * (1.0 / NOPE)
        sq_rope = jnp.where(is_nope, 0.0, jnp.where(lane < QK_HD, sq, 0.0))
        sr = jnp.sum(sq_rope, axis=-1, keepdims=True) * (1.0 / ROPE)
        scale = jnp.where(is_nope, lax.rsqrt(sn + NORM_EPS), lax.rsqrt(sr + NORM_EPS))
        xr = qh * scale * gqn_ref[...]
        out = _rope_apply(xr, cq_ref[...], s1q_ref[...], s2q_ref[...])
        q_ref[0, :, h * HEAD_PAD:(h + 1) * HEAD_PAD] = (out * sm_scale).astype(BF16)
    zkv = z[:, Q_PAD:Q_PAD + KV_RANK]
    ckv_ref[0] = zkv * lax.rsqrt(jnp.mean(zkv * zkv, axis=-1, keepdims=True) + NORM_EPS) * gkva_ref[...]
    zkr = z[:, Q_PAD + KV_RANK:]
    xr = zkr * lax.rsqrt(jnp.sum(zkr * zkr, axis=-1, keepdims=True) * (1.0 / ROPE) + NORM_EPS) * gkr_ref[...]
    kr_ref[0] = _rope_apply(xr, ck_ref[...], s1k_ref[...], s2k_ref[...])


def _mla_q(zb, tabs, lp):
    b, t, _ = zb.shape
    tm = min(512, t)
    tab = pl.BlockSpec((tm, LANE), lambda i, j: (j, 0))
    row = lambda w: _full((1, w))
    return pl.pallas_call(
        _mla_q_kernel,
        grid=(b, t // tm),
        in_specs=[pl.BlockSpec((1, tm, 512), lambda i, j: (i, j, 0)), tab, tab, tab, tab, tab, tab,
                  row(Q_PAD), _full((Q_PAD, HEADS * HEAD_PAD)), row(HEAD_PAD), row(KV_RANK), row(LANE)],
        out_specs=[pl.BlockSpec((1, tm, HEADS * HEAD_PAD), lambda i, j: (i, j, 0)),
                   pl.BlockSpec((1, tm, KV_RANK), lambda i, j: (i, j, 0)),
                   pl.BlockSpec((1, tm, LANE), lambda i, j: (i, j, 0))],
        out_shape=[jax.ShapeDtypeStruct((b, t, HEADS * HEAD_PAD), BF16),
                   jax.ShapeDtypeStruct((b, t, KV_RANK), F32),
                   jax.ShapeDtypeStruct((b, t, LANE), F32)],
        compiler_params=_params(("parallel", "parallel")),
        name="mla_q",
    )(zb, *tabs, lp["mla_qa_g"], lp["mla_w_uq"], lp["mla_qn_g"], lp["mla_kva_g"], lp["mla_kr_g"])


def _mla_kv_kernel(ckv_ref, kr_ref, wuk_ref, gkn_ref, wuv_ref, k_ref, v_ref, *, transposed):
    cb = ckv_ref[0].astype(BF16)
    kn = jnp.dot(cb, wuk_ref[...], preferred_element_type=F32)
    krs = pltpu.roll(kr_ref[0], NOPE, 1)
    for h in range(HEADS):
        kh = kn[:, h * HEAD_PAD:(h + 1) * HEAD_PAD]
        ms = jnp.sum(kh * kh, axis=-1, keepdims=True) * (1.0 / NOPE)
        k_ref[0, :, h * HEAD_PAD:(h + 1) * HEAD_PAD] = (kh * lax.rsqrt(ms + NORM_EPS) * gkn_ref[...] + krs).astype(BF16)
    if transposed:
        v = lax.dot_general(wuv_ref[...], cb, (((1,), (1,)), ((), ())), preferred_element_type=F32)
        ones = jnp.ones((VT_ONES, v.shape[1]), F32)
        v = jnp.concatenate([x for h in range(HEADS) for x in (v[h * HD:(h + 1) * HD], ones)], axis=0)
    else:
        v = jnp.dot(cb, wuv_ref[...], preferred_element_type=F32)
    v_ref[0] = v.astype(BF16)


def _mla_kv(ckv, kr, lp, *, transposed):
    b, n, _ = ckv.shape
    tm = 512 if n % 512 == 0 else n
    wide = HEADS * HEAD_PAD
    if transposed:
        wuv, v_spec = lp["mla_w_uv_t"], pl.BlockSpec((1, HEADS * VT_ROWS, tm), lambda i, j: (i, 0, j))
        v_shape = jax.ShapeDtypeStruct((b, HEADS * VT_ROWS, n), BF16)
    else:
        wuv, v_spec = lp["mla_w_uv"], pl.BlockSpec((1, tm, wide), lambda i, j: (i, j, 0))
        v_shape = jax.ShapeDtypeStruct((b, n, wide), BF16)
    return pl.pallas_call(
        functools.partial(_mla_kv_kernel, transposed=transposed),
        grid=(b, n // tm),
        in_specs=[pl.BlockSpec((1, tm, KV_RANK), lambda i, j: (i, j, 0)),
                  pl.BlockSpec((1, tm, LANE), lambda i, j: (i, j, 0)),
                  _full((KV_RANK, wide)), _full((1, HEAD_PAD)), _full(wuv.shape)],
        out_specs=[pl.BlockSpec((1, tm, wide), lambda i, j: (i, j, 0)), v_spec],
        out_shape=[jax.ShapeDtypeStruct((b, n, wide), BF16), v_shape],
        compiler_params=_params(("parallel", "parallel")),
        name="mla_kv",
    )(ckv, kr, lp["mla_w_uk"], lp["mla_kn_g"], wuv)


VT_ONES = 16
VT_ROWS = HD + VT_ONES
FLASH_AHEAD = 2
FLASH_BEHIND = 1


def _col_reduce(x, op, final):
    n = x.shape[0]
    step = n & -n
    parts = [x[i:i + step] for i in range(0, n, step)]
    while len(parts) > 1:
        parts = [op(parts[i], parts[i + 1]) if i + 1 < len(parts) else parts[i] for i in range(0, len(parts), 2)]
    x = parts[0]
    while x.shape[0] > 8:
        half = x.shape[0] // 2
        x = op(x[:half], x[half:])
    return final(x, axis=0, keepdims=True)


def _flash_t_kernel(qt_ref, kt_ref, q_ref, k_ref, vt_ref, o_ref, m_scr, acc_scr, *, tile, qb):
    p_id = pl.program_id(1)
    qi = qt_ref[p_id]
    ki = kt_ref[p_id]
    nrb = tile // qb
    nt = (((1,), (1,)), ((), ()))

    @pl.when(ki == 0)
    def _():
        m_scr[...] = jnp.full(m_scr.shape, -1e30, F32)
        acc_scr[...] = jnp.zeros(acc_scr.shape, F32)

    def block(h, rb, n_keys, masked_tail):
        sl = slice(h * HEAD_PAD, (h + 1) * HEAD_PAD)
        qh = q_ref[0, rb * qb:(rb + 1) * qb, sl]
        s = lax.dot_general(k_ref[0, 0:n_keys, sl], qh, nt, preferred_element_type=F32)
        if masked_tail:
            kr = lax.broadcasted_iota(jnp.int32, (qb, qb), 0)
            qc = lax.broadcasted_iota(jnp.int32, (qb, qb), 1)
            tail = jnp.where((kr // CHUNK) <= (qc // CHUNK), s[n_keys - qb:], -1e30)
            s = tail if n_keys == qb else jnp.concatenate([s[:n_keys - qb], tail], axis=0)
        yield
        idx = h * nrb + rb
        m_prev = m_scr[idx]
        m_new = jnp.maximum(m_prev, _col_reduce(s, jnp.maximum, jnp.max))
        alpha = jnp.exp2(m_prev - m_new)
        m_scr[idx] = m_new
        pb = jnp.exp2(s - m_new).astype(BF16)
        yield
        pv = jnp.dot(vt_ref[0, h * VT_ROWS:(h + 1) * VT_ROWS, 0:n_keys], pb, preferred_element_type=F32)
        acc_scr[idx] = alpha * acc_scr[idx] + pv

    def run_staggered(gens):
        n = len(gens)
        for j in range(n + FLASH_AHEAD + FLASH_BEHIND):
            for lag in (0, FLASH_AHEAD, FLASH_AHEAD + FLASH_BEHIND):
                if 0 <= j - lag < n:
                    next(gens[j - lag], None)

    @pl.when(ki < qi)
    def _():
        run_staggered([block(h, rb, tile, False) for rb in range(nrb) for h in range(HEADS)])

    @pl.when(ki == qi)
    def _():
        run_staggered([block(h, rb, (rb + 1) * qb, True) for rb in range(nrb) for h in range(HEADS)])
        for rb in range(nrb):
            o_t = jnp.concatenate([acc_scr[h * nrb + rb, 0:HD] / acc_scr[h * nrb + rb, HD:HD + 1]
                                   for h in range(HEADS)], axis=0)
            o_ref[0, rb * qb:(rb + 1) * qb, :] = o_t.T


def _flash_t(q, k, vt):
    b, t, wide = q.shape
    tile = min(1024, t)
    qb = min(256, tile)
    nrb = tile // qb
    pairs = [(i, j) for i in range(t // tile) for j in range(i + 1)]
    qt = jnp.asarray(np.array([p[0] for p in pairs], np.int32))
    kt = jnp.asarray(np.array([p[1] for p in pairs], np.int32))
    grid_spec = pltpu.PrefetchScalarGridSpec(
        num_scalar_prefetch=2,
        grid=(b, len(pairs)),
        in_specs=[pl.BlockSpec((1, tile, wide), lambda i, p, qt, kt: (i, qt[p], 0)),
                  pl.BlockSpec((1, tile, wide), lambda i, p, qt, kt: (i, kt[p], 0)),
                  pl.BlockSpec((1, HEADS * VT_ROWS, tile), lambda i, p, qt, kt: (i, 0, kt[p]))],
        out_specs=pl.BlockSpec((1, tile, BRANCH), lambda i, p, qt, kt: (i, qt[p], 0)),
        scratch_shapes=[pltpu.VMEM((HEADS * nrb, 1, qb), F32), pltpu.VMEM((HEADS * nrb, VT_ROWS, qb), F32)],
    )
    return pl.pallas_call(
        functools.partial(_flash_t_kernel, tile=tile, qb=qb),
        grid_spec=grid_spec,
        out_shape=jax.ShapeDtypeStruct((b, t, BRANCH), F32),
        compiler_params=_params(("parallel", "arbitrary")),
        name="flash_t",
    )(qt, kt, q, k, vt)


def _attend_cache_kernel(q_ref, k_ref, v_ref, o_ref, *, kv_valid):
    tq, tk = q_ref.shape[1], k_ref.shape[1]
    valid = lax.broadcasted_iota(jnp.int32, (tq, tk), 1) < kv_valid
    outs = []
    for h in range(HEADS):
        sl = slice(h * HEAD_PAD, (h + 1) * HEAD_PAD)
        s = lax.dot_general(q_ref[0, :, sl], k_ref[0, :, sl], (((1,), (1,)), ((), ())), preferred_element_type=F32)
        s = jnp.where(valid, s, -1e30)
        p = jnp.exp2(s - jnp.max(s, axis=-1, keepdims=True))
        pv = jnp.dot(p.astype(BF16), v_ref[0, :, sl], preferred_element_type=F32)
        outs.append(pv / jnp.sum(p, axis=-1, keepdims=True))
    o_ref[0, :, 0:LANE] = outs[0] + pltpu.roll(outs[1], HD, 1)
    o_ref[0, :, LANE:2 * LANE] = outs[2] + pltpu.roll(outs[3], HD, 1)


def _attend_cache(q, k, v, *, kv_valid):
    b, t, wide = q.shape
    n = k.shape[1]
    assert (t - 1) // CHUNK == 0 and (kv_valid - t) % CHUNK == 0, "queries must sit in the last key chunk"
    return pl.pallas_call(
        functools.partial(_attend_cache_kernel, kv_valid=kv_valid),
        grid=(b,),
        in_specs=[pl.BlockSpec((1, t, wide), lambda i: (i, 0, 0)), pl.BlockSpec((1, n, wide), lambda i: (i, 0, 0)),
                  pl.BlockSpec((1, n, wide), lambda i: (i, 0, 0))],
        out_specs=pl.BlockSpec((1, t, BRANCH), lambda i: (i, 0, 0)),
        out_shape=jax.ShapeDtypeStruct((b, t, BRANCH), F32),
        compiler_params=_params(("parallel",)),
        name="attend_cache",
    )(q, k, v)


def _combine_kernel(x_ref, ya_ref, yb_ref, zc_ref, zd_ref, halo_ref, zg_ref, lng_ref, lnb_ref, ws_ref, sb_ref,
                    pw_ref, psc_ref, wout_ref, *out_refs, tm, cs, n_hist, halo_is_prev_tile):
    out_ref = out_refs[0]
    ti = pl.program_id(1)
    lane = lax.broadcasted_iota(jnp.int32, (1, BRANCH), 1)

    zc = zc_ref[0]
    u = zc[:, 0:BRANCH]
    vraw = zc[:, BRANCH:]
    xc = vraw - jnp.mean(vraw, axis=-1, keepdims=True)
    var = jnp.mean(xc * xc, axis=-1, keepdims=True)
    vn = xc * lax.rsqrt(var + SGU_EPS) * lng_ref[...] + lnb_ref[...]
    if len(out_refs) > 1:
        out_refs[1][0] = vn
    vnb = vn.astype(BF16)
    zero = jnp.zeros((), BF16)
    parts = []
    for n in range(tm // cs):
        vch = vnb[n * cs:(n + 1) * cs]
        stacked = jnp.concatenate([jnp.where(lane // HD == h, vch, zero) for h in range(HEADS)], axis=0)
        parts.append(jnp.dot(ws_ref[...], stacked, preferred_element_type=F32) + sb_ref[...])
    yc = u * (parts[0] if len(parts) == 1 else jnp.concatenate(parts, axis=0))

    zd = zd_ref[0]
    halo = halo_ref[0]
    if halo_is_prev_tile:
        halo = jnp.where(ti == 0, 0.0, halo)
    ext = jnp.concatenate([halo, zd], axis=0)
    sums = []
    acc = ext
    for sh in (1, 2, 4, 8):
        acc = acc + pltpu.roll(acc, sh, 0)
        sums.append(acc[HALO:])
    t_idx = lax.broadcasted_iota(jnp.int32, (tm, BRANCH), 0) + (ti * tm + n_hist + 1)
    lane2 = lax.broadcasted_iota(jnp.int32, (tm, BRANCH), 1)
    win = jnp.where(lane2 < HD, POOL_WINDOWS[0],
                    jnp.where(lane2 < 2 * HD, POOL_WINDOWS[1],
                              jnp.where(lane2 < 3 * HD, POOL_WINDOWS[2], POOL_WINDOWS[3])))
    cnt = jnp.minimum(t_idx, win).astype(F32)
    ssum = jnp.where(lane2 < HD, sums[0],
                     jnp.where(lane2 < 2 * HD, sums[1], jnp.where(lane2 < 3 * HD, sums[2], sums[3])))
    d = ssum / cnt - zd
    yd = jnp.dot(d.astype(BF16), pw_ref[...], preferred_element_type=F32) * psc_ref[...]

    g = zg_ref[0]
    gate = g / (1.0 + jnp.exp(-g))
    y = jnp.concatenate([ya_ref[0], yb_ref[0], yc, yd], axis=-1) * gate
    out_ref[0] = x_ref[0] + jnp.dot(y.astype(BF16), wout_ref[...], preferred_element_type=F32)


def _combine(x, ya, yb, zc, zd, halo, zg, lp, *, n_hist, want_vn):
    b, t, _ = x.shape
    tm = min(512, t)
    cs = min(SGU_CHUNK, t)
    halo_is_prev_tile = halo is None
    if halo_is_prev_tile:
        halo = zd
        per = tm // HALO
        halo_spec = pl.BlockSpec((1, HALO, BRANCH), lambda i, j: (i, jnp.maximum(j * per - 1, 0), 0))
    else:
        halo_spec = pl.BlockSpec((1, HALO, BRANCH), lambda i, j: (i, 0, 0))
    blk = lambda w: pl.BlockSpec((1, tm, w), lambda i, j: (i, j, 0))
    row = lambda w: _full((1, w))
    out_specs = [blk(D_MODEL)]
    out_shape = [jax.ShapeDtypeStruct((b, t, D_MODEL), F32)]
    if want_vn:
        out_specs.append(blk(BRANCH))
        out_shape.append(jax.ShapeDtypeStruct((b, t, BRANCH), F32))
    res = pl.pallas_call(
        functools.partial(_combine_kernel, tm=tm, cs=cs, n_hist=n_hist, halo_is_prev_tile=halo_is_prev_tile),
        grid=(b, t // tm),
        in_specs=[blk(D_MODEL), blk(BRANCH), blk(BRANCH), blk(2 * BRANCH), blk(BRANCH), halo_spec, blk(D_MODEL),
                  row(BRANCH), row(BRANCH), _full((cs, HEADS * cs)), _full((cs, BRANCH)),
                  _full((BRANCH, BRANCH)), row(BRANCH), _full((D_MODEL, D_MODEL))],
        out_specs=out_specs,
        out_shape=out_shape,
        compiler_params=_params(("parallel", "parallel")),
        name="combine",
    )(x, ya, yb, zc, zd, halo, zg, lp["sgu_ln_g"], lp["sgu_ln_b"], lp["sgu_w_cat"][cs], lp["sgu_b_tab"][cs],
      lp["pool_w_bd"], lp["pool_scale"], lp["w_out"])
    return res


def _pad_cols(w, total):
    return jnp.pad(w, ((0, 0), (0, total - w.shape[1])))


def _head_slots(w, real):
    rows = w.shape[0]
    w = w.reshape(rows, HEADS, real)
    return jnp.pad(w, ((0, 0), (0, 0), (0, HEAD_PAD - real))).reshape(rows, HEADS * HEAD_PAD)


def _layer_params(l, norm_g, w_in, w_out, rw_mu, rw_w0, rw_w2, rw_a0, rw_a2, rw_kk, rw_ka, rw_rk, rw_gn_g,
                  rw_gn_b, mla_qa_g, mla_w_uq, mla_kva_g, mla_w_uk, mla_w_uv, mla_q_norm_g, mla_k_norm_g, sgu_w,
                  sgu_b, sgu_ln_g, sgu_ln_b, pool_w, pool_scale, sgu_sizes):
    wi = w_in[l]
    off_b = RW_SHIFT
    off_c = off_b + Q_RANK + KV_RANK + ROPE
    off_d = off_c + 2 * BRANCH
    off_g = off_d + BRANCH
    w_in_r = jnp.concatenate([
        wi[:, :off_b],
        _pad_cols(wi[:, off_b:off_b + Q_RANK], Q_PAD),
        wi[:, off_b + Q_RANK:off_b + Q_RANK + KV_RANK],
        _pad_cols(wi[:, off_b + Q_RANK + KV_RANK:off_c], LANE),
        wi[:, off_c:off_d], wi[:, off_d:off_g], wi[:, off_g:]], axis=1).astype(BF16)
    zeros = jnp.zeros((RW_LORA, BRANCH), F32)
    rw_lora = jnp.concatenate([jnp.concatenate([rw_w2[l], zeros], 1), jnp.concatenate([zeros, rw_a2[l]], 1)], 0)
    row = lambda v: v.reshape(1, -1).astype(F32)
    qn = mla_q_norm_g[l]
    kn = mla_k_norm_g[l]
    tri = jnp.tril(jnp.ones((SGU_CHUNK, SGU_CHUNK), bool))
    ws = jnp.where(tri[None], sgu_w[l], 0.0)
    sgu_w_cat = {cs: jnp.concatenate([ws[h, :cs, :cs] for h in range(HEADS)], axis=1).astype(BF16)
                 for cs in sgu_sizes}
    sgu_b_tab = {cs: jnp.repeat(jnp.transpose(sgu_b[l])[:cs], HD, axis=1) for cs in sgu_sizes}
    eye = jnp.eye(HEADS, dtype=F32)
    pool_w_bd = jnp.einsum("gcd,gh->gchd", pool_w[l], eye).reshape(BRANCH, BRANCH).astype(BF16)
    return {
        "norm_g": row(norm_g[l]), "w_in": w_in_r, "w_out": w_out[l].astype(BF16),
        "rw_mu": row(rw_mu[l]), "rw_lora": rw_lora.astype(BF16), "rw_w0": row(rw_w0[l]), "rw_a0": row(rw_a0[l]),
        "rw_kk": row(rw_kk[l]), "rw_ka": row(rw_ka[l]), "rw_rk": row(rw_rk[l]), "rw_gn_g": row(rw_gn_g[l]),
        "rw_gn_b": row(rw_gn_b[l]),
        "mla_qa_g": _pad_cols(row(mla_qa_g[l]), Q_PAD),
        "mla_w_uq": jnp.pad(_head_slots(mla_w_uq[l], QK_HD), ((0, Q_PAD - Q_RANK), (0, 0))).astype(BF16),
        "mla_qn_g": _pad_cols(row(qn), HEAD_PAD),
        "mla_kva_g": row(mla_kva_g[l]),
        "mla_kr_g": _pad_cols(row(kn[NOPE:]), LANE),
        "mla_w_uk": _head_slots(mla_w_uk[l], NOPE).astype(BF16),
        "mla_kn_g": _pad_cols(row(kn[:NOPE]), HEAD_PAD),
        "mla_w_uv": _head_slots(mla_w_uv[l], HD).astype(BF16),
        "mla_w_uv_t": jnp.transpose(mla_w_uv[l]).astype(BF16),
        "sgu_ln_g": row(sgu_ln_g[l]), "sgu_ln_b": row(sgu_ln_b[l]), "sgu_w_cat": sgu_w_cat, "sgu_b_tab": sgu_b_tab,
        "pool_w_bd": pool_w_bd, "pool_scale": row(pool_scale[l]),
    }


def _rope_tables(pos):
    half = ROPE // 2
    inv = ROPE_THETA ** (-jnp.arange(half, dtype=F32) / half)
    ang = pos.astype(F32)[:, None] * inv[None, :]
    cos, sin = jnp.cos(ang), jnp.sin(ang)
    t = pos.shape[0]

    def place(first, second, off, fill):
        base = jnp.full((t, LANE), fill, F32)
        base = lax.dynamic_update_slice(base, first, (0, off))
        return lax.dynamic_update_slice(base, second, (0, off + half))

    z = jnp.zeros_like(sin)
    tabs = []
    for off, fill in ((NOPE, 1.0), (0, 0.0)):
        tabs += [place(cos, cos, off, fill), place(-sin, z, off, 0.0), place(z, sin, off, 0.0)]
    return tabs


def _state_to_blockdiag(s):
    b = s.shape[0]
    eye = jnp.eye(HEADS, dtype=s.dtype)
    return jnp.einsum("bhij,hg->bhjgi", s, eye).reshape(b, BRANCH, BRANCH)


def _blockdiag_to_state(z):
    b = z.shape[0]
    z5 = z.reshape(b, HEADS, HD, HEADS, HD)
    return jnp.stack([jnp.swapaxes(z5[:, h, :, h, :], 1, 2) for h in range(HEADS)], axis=1)


def _layer(x, lp, tabs, hist):
    b, t, _ = x.shape
    za, zb, zc, zd, zg = _inproj(x.reshape(b * t, D_MODEL), lp["norm_g"], lp["w_in"])
    za = za.reshape(b, t, RW_SHIFT)
    zb = zb.reshape(b, t, 512)
    zc = zc.reshape(b, t, 2 * BRANCH)
    zd = zd.reshape(b, t, BRANCH)
    zg = zg.reshape(b, t, D_MODEL)
    if hist is None:
        shift_prev = jnp.zeros((b, 1, RW_SHIFT), F32)
        zbd0 = jnp.zeros((b, BRANCH, BRANCH), F32)
    else:
        c_ckv, c_kr, s_wkv, s_shift, s_pool = hist
        shift_prev = s_shift[:, None, :]
        zbd0 = _state_to_blockdiag(s_wkv)
    ya, zfin = _rwkv(za, shift_prev, zbd0, lp)
    q, ckv, krp = _mla_q(zb, tabs, lp)
    if hist is None:
        k, vt = _mla_kv(ckv, krp, lp, transposed=True)
        yb = _flash_t(q, k, vt)
        halo, n_hist = None, 0
        pool_src = zd
    else:
        past = c_ckv.shape[1]
        n_all = past + t
        n_pad = -(-n_all // LANE) * LANE
        ckv_all = jnp.pad(jnp.concatenate([c_ckv, ckv], axis=1), ((0, 0), (0, n_pad - n_all), (0, 0)))
        kr_all = jnp.concatenate([jnp.pad(c_kr, ((0, 0), (0, 0), (0, LANE - ROPE))), krp], axis=1)
        kr_all = jnp.pad(kr_all, ((0, 0), (0, n_pad - n_all), (0, 0)))
        k, v = _mla_kv(ckv_all, kr_all, lp, transposed=False)
        yb = _attend_cache(q, k, v, kv_valid=n_all)
        halo = jnp.pad(s_pool, ((0, 0), (HALO - POOL_HIST, 0), (0, 0)))
        n_hist = POOL_HIST
        pool_src = jnp.concatenate([s_pool, zd], axis=1)
    res = _combine(x, ya, yb, zc, zd, halo, zg, lp, n_hist=n_hist, want_vn=hist is not None)
    state = (ckv, krp[..., :ROPE], _blockdiag_to_state(zfin), za[:, -1], pool_src[:, pool_src.shape[1] - POOL_HIST:])
    if hist is None:
        return res[0], state
    return res[0], state + (res[1],)


def kernel(x_prompt, x_sample, cache_ckv, cache_krope, state_wkv, state_shift, state_pool, norm_g, w_in, w_out, rw_mu, rw_w0, rw_w2, rw_a0, rw_a2, rw_kk, rw_ka, rw_rk, rw_gn_g, rw_gn_b, mla_qa_g, mla_w_uq, mla_kva_g, mla_w_uk, mla_w_uv, mla_q_norm_g, mla_k_norm_g, sgu_w, sgu_b, sgu_ln_g, sgu_ln_b, pool_w, pool_scale):
    depth = w_in.shape[0]
    t_p = x_prompt.shape[1]
    t_s = x_sample.shape[1]
    past = cache_ckv.shape[2]
    sgu_sizes = sorted({min(SGU_CHUNK, t_p), min(SGU_CHUNK, t_s)})
    tabs_p = _rope_tables(jnp.arange(t_p))
    tabs_s = _rope_tables(past + jnp.arange(t_s))
    yp, ys = x_prompt, x_sample
    new_p = [[] for _ in range(5)]
    new_s = [[] for _ in range(6)]
    for l in range(depth):
        lp = _layer_params(l, norm_g, w_in, w_out, rw_mu, rw_w0, rw_w2, rw_a0, rw_a2, rw_kk, rw_ka, rw_rk, rw_gn_g,
                           rw_gn_b, mla_qa_g, mla_w_uq, mla_kva_g, mla_w_uk, mla_w_uv, mla_q_norm_g, mla_k_norm_g,
                           sgu_w, sgu_b, sgu_ln_g, sgu_ln_b, pool_w, pool_scale, sgu_sizes)
        yp, st_p = _layer(yp, lp, tabs_p, None)
        ys, st_s = _layer(ys, lp, tabs_s, (cache_ckv[l], cache_krope[l], state_wkv[l], state_shift[l], state_pool[l]))
        for lst, arr in zip(new_p, st_p):
            lst.append(arr)
        for lst, arr in zip(new_s, st_s):
            lst.append(arr)
    outs_p = [jnp.stack(a, axis=0) for a in new_p]
    outs_s = [jnp.stack(a, axis=0) for a in new_s]
    return (yp, ys, *outs_p, *outs_s)
```

```python
import functools
import math

import jax
import jax.numpy as jnp
import numpy as np
from jax import lax
from jax.experimental import pallas as pl
from jax.experimental.pallas import tpu as pltpu

F32 = jnp.float32
BF16 = jnp.bfloat16

D_MODEL = 1024
CHUNK = 64
BRANCH = 256
NORM_EPS = 1e-6
HEADS = 4
HD = 64
RW_LORA = 64
RW_SHIFT = 3 * BRANCH + 2 * RW_LORA
RW_GN_EPS = 64e-5
NOPE = 64
ROPE = 32
QK_HD = NOPE + ROPE
Q_RANK = 192
KV_RANK = 128
ROPE_THETA = 10000.0
SGU_CHUNK = 128
SGU_EPS = 1e-5
POOL_WINDOWS = (2, 4, 8, 16)
POOL_HIST = 15
D_IN = 3040

LANE = 128
HALO = 16
Q_PAD = 256
HEAD_PAD = 128
VMEM_LIMIT = 48 * 1024 * 1024

SEG_A = (0, 896)
SEG_B = (896, 1408)
SEG_C = (1408, 1920)
SEG_D = (1920, 2176)
SEG_G = (2176, 3200)
D_IN_PAD = 3200


def _params(sem):
    return pltpu.CompilerParams(dimension_semantics=sem, vmem_limit_bytes=VMEM_LIMIT)


def _full(shape):
    nd = len(shape)
    return pl.BlockSpec(shape, lambda *_: (0,) * nd)


def _inproj_kernel(x_ref, g_ref, w_ref, za_ref, zb_ref, zc_ref, zd_ref, zg_ref):
    x = x_ref[...]
    ms = jnp.mean(x * x, axis=-1, keepdims=True)
    xn = (x * lax.rsqrt(ms + NORM_EPS) * g_ref[...]).astype(BF16)
    for ref, (lo, hi) in ((za_ref, SEG_A), (zb_ref, SEG_B), (zc_ref, SEG_C), (zd_ref, SEG_D), (zg_ref, SEG_G)):
        ref[...] = jnp.dot(xn, w_ref[:, lo:hi], preferred_element_type=F32)


def _inproj(x2d, g, w):
    n = x2d.shape[0]
    tm = min(512, n)
    widths = [hi - lo for lo, hi in (SEG_A, SEG_B, SEG_C, SEG_D, SEG_G)]
    return pl.pallas_call(
        _inproj_kernel,
        grid=(n // tm,),
        in_specs=[pl.BlockSpec((tm, D_MODEL), lambda i: (i, 0)), _full((1, D_MODEL)), _full((D_MODEL, D_IN_PAD))],
        out_specs=[pl.BlockSpec((tm, wd), lambda i: (i, 0)) for wd in widths],
        out_shape=[jax.ShapeDtypeStruct((n, wd), F32) for wd in widths],
        compiler_params=_params(("parallel",)),
        name="inproj",
    )(x2d, g, w)


def _softplus(x):
    return jnp.maximum(x, 0.0) + jnp.log(1.0 + jnp.exp(-jnp.abs(x)))


def _split_bf16(x, parts):
    out = []
    for _ in range(parts - 1):
        hi = x.astype(BF16)
        out.append(hi)
        x = x - hi.astype(F32)
    out.append(x.astype(BF16))
    return out


def _head_sum(x, ones_bd):
    return sum(jnp.dot(p, ones_bd, preferred_element_type=F32) for p in _split_bf16(x, 2))


def _dot_s(a, b):
    return jnp.dot(a.astype(BF16), b.astype(BF16), preferred_element_type=F32)


def _dot_nt_s(a, b):
    return lax.dot_general(a.astype(BF16), b.astype(BF16), (((1,), (1,)), ((), ())), preferred_element_type=F32)


def _rwkv_kernel(za_ref, shift_ref, z0_ref, mu_ref, wl_ref, w0_ref, a0_ref, kkg_ref, kag_ref, rkg_ref,
                 gng_ref, gnb_ref, ya_ref, zfin_ref, z_scr, carry_scr, *, c, levels, rows):
    ci = pl.program_id(1)

    @pl.when(ci == 0)
    def _():
        z_scr[...] = z0_ref[...]
        for i in range(rows):
            carry_scr[i] = jnp.broadcast_to(shift_ref[i], carry_scr.shape[1:])

    subs = za_ref.shape[1] // c
    states = {}
    consts = _rwkv_consts(c)
    gens, starts = [], []
    for s in range(subs):
        for i in range(rows):
            za = za_ref[i, s * c:(s + 1) * c, :]
            carry = carry_scr[i, 0:1, :] if s == 0 else za_ref[i, s * c - 1:s * c, :]
            get_z = (lambda i=i: z_scr[i]) if s == 0 else (lambda i=i, s=s: states[(i, s - 1)])
            publish = lambda z_new, i=i, s=s: states.__setitem__((i, s), z_new)
            gens.append(_rwkv_chunk(za, carry, get_z, publish, consts, mu_ref, wl_ref, w0_ref, a0_ref, kkg_ref, kag_ref,
                                    rkg_ref, gng_ref, gnb_ref, c=c, levels=levels))
            starts.append(s * RWKV_SKEW)
    yas = _run_lockstep(gens, starts)
    ya_ref[...] = jnp.stack([jnp.concatenate([yas[s * rows + i] for s in range(subs)], axis=0)
                             for i in range(rows)], axis=0)
    z_scr[...] = jnp.stack([states[(i, subs - 1)] for i in range(rows)], axis=0)
    carry_scr[...] = jnp.stack([jnp.broadcast_to(za_ref[i, subs * c - 1:subs * c, :], carry_scr.shape[1:])
                                for i in range(rows)], axis=0)

    @pl.when(ci == pl.num_programs(1) - 1)
    def _():
        zfin_ref[...] = z_scr[...]


def _rwkv_consts(c):
    hc = HEADS * c
    lane = lax.broadcasted_iota(jnp.int32, (c, BRANCH), 1)
    rr = lax.broadcasted_iota(jnp.int32, (BRANCH, BRANCH), 0)
    cc = lax.broadcasted_iota(jnp.int32, (BRANCH, BRANCH), 1)
    tr = lax.broadcasted_iota(jnp.int32, (c, c), 0)
    tc = lax.broadcasted_iota(jnp.int32, (c, c), 1)
    wt = lax.broadcasted_iota(jnp.int32, (c, hc), 0)
    wl = lax.broadcasted_iota(jnp.int32, (c, hc), 1)
    ws = wl % c
    bits = lambda cond: jnp.where(cond, 1.0, 0.0).astype(BF16)
    return {
        "head_block": rr // HD == cc // HD,
        "ones_bd": bits(rr // HD == cc // HD),
        "tril": bits(tc <= tr),
        "head_lanes": [bits(lane // HD == h) for h in range(HEADS)],
        "head_cols": [bits(wl // c == h) for h in range(HEADS)],
        "strict": ws < wt, "incl": ws <= wt,
        "eye_w": jnp.where(ws == wt, 1.0, 0.0).astype(F32),
    }


def _rwkv_chunk(za, carry, get_z, publish, consts, mu_ref, wl_ref, w0_ref, a0_ref, kkg_ref, kag_ref, rkg_ref,
                gng_ref, gnb_ref, *, c, levels):
    row = lax.broadcasted_iota(jnp.int32, za.shape, 0)
    prev = jnp.where(row == 0, carry, pltpu.roll(za, 1, 0))
    zs = za + mu_ref[...] * (prev - za)
    r = zs[:, 0:BRANCH]
    k = zs[:, BRANCH:2 * BRANCH]
    v = zs[:, 2 * BRANCH:3 * BRANCH]
    wa = zs[:, 3 * BRANCH:RW_SHIFT]
    lane128 = lax.broadcasted_iota(jnp.int32, wa.shape, 1)
    lora_in = jnp.where(lane128 < RW_LORA, jnp.tanh(wa), wa).astype(BF16)
    lora = jnp.dot(lora_in, wl_ref[...], preferred_element_type=F32)
    w_log = -_softplus(-(w0_ref[...] + lora[:, :BRANCH])) - 0.5
    logd = -jnp.exp(w_log)
    asig = 1.0 / (1.0 + jnp.exp(-(a0_ref[...] + lora[:, BRANCH:])))

    ones_bd = consts["ones_bd"]

    kkv = k * kkg_ref[...]
    kk = kkv * lax.rsqrt(jnp.maximum(_head_sum(kkv * kkv, ones_bd), 1e-24))
    kmod = k * (1.0 + (asig - 1.0) * kag_ref[...])
    avec = -kk
    bvec = kk * asig

    lcum = sum(jnp.dot(consts["tril"], p, preferred_element_type=F32) for p in _split_bf16(logd, 3))
    lexc = lcum - logd
    ltot = lcum[c - 1:c, :]
    g_in = jnp.exp(lcum)
    g_ex = jnp.exp(lexc)
    g_inv = jnp.exp(-lcum)
    g_rem = jnp.exp(ltot - lcum)

    def stack(x):
        xb = x.astype(BF16)
        return jnp.concatenate([xb * m for m in consts["head_lanes"]], axis=0)

    strict = lambda x: jnp.where(consts["strict"], x, 0.0)
    incl = lambda x: jnp.where(consts["incl"], x, 0.0)

    def stack_w(x):
        xb = x.astype(BF16)
        return jnp.concatenate([xb * m for m in consts["head_cols"]], axis=0)

    a_t = avec * g_ex
    r_t = r * g_in
    b_s = stack(bvec * g_inv)
    k_s = stack(kmod * g_inv)
    v_s = stack(v)
    yield
    a_ab = strict(_dot_nt_s(a_t, b_s))
    a_ak = strict(_dot_nt_s(a_t, k_s))
    a_rb = incl(_dot_nt_s(r_t, b_s))
    a_rk = incl(_dot_nt_s(r_t, k_s))
    yield

    t_w = consts["eye_w"] + a_ab
    pw = a_ab
    bd = stack_w(pw)
    for _ in range(levels - 1):
        pw = _dot_s(pw, bd)
        yield
        bd = stack_w(pw)
        t_w = t_w + _dot_s(t_w, bd)

    g_col = jnp.exp(jnp.sum(logd.T, axis=1, keepdims=True))
    pad = [jnp.zeros((max(LANE - 2 * c, 0) // 2, BRANCH), F32)] * 2 if 2 * c < LANE else []
    lhs_t = jnp.concatenate([bvec * g_rem, kmod * g_rem] + pad, axis=0).T
    yield
    z = get_z()
    w_z = _dot_s(a_t, z) + _dot_s(a_ak, v_s)
    y_part = _dot_s(r_t, z) + _dot_s(a_rk, v_s)
    yield
    u = _dot_s(t_w, stack(w_z))
    yield
    y = y_part + _dot_s(a_rb, stack(u))
    upd = _dot_s(lhs_t, jnp.concatenate([u, v] + pad, axis=0))
    z_new = z * g_col + jnp.where(consts["head_block"], upd, 0.0)
    publish(z_new)
    yield

    mean = _head_sum(y, ones_bd) * (1.0 / HD)
    yield
    yc = y - mean
    var = _head_sum(yc * yc, ones_bd) * (1.0 / HD)
    yn = yc * lax.rsqrt(var + RW_GN_EPS) * gng_ref[...] + gnb_ref[...]
    bonus = _head_sum(r * kmod * rkg_ref[...], ones_bd) * v
    return yn + bonus


def _run_lockstep(gens, starts):
    results = [None] * len(gens)
    done = [False] * len(gens)
    tick = 0
    while not all(done):
        for i, g in enumerate(gens):
            if done[i] or tick < starts[i]:
                continue
            try:
                next(g)
            except StopIteration as stop:
                results[i], done[i] = stop.value, True
        tick += 1
    return results


RWKV_SKEW = 3
RWKV_ROWS = 4
RWKV_SUBS = 2


def _rwkv(za, shift_prev, zbd0, lp):
    b, t, _ = za.shape
    c = min(CHUNK, t)
    levels = int(math.log2(c))
    rows = math.gcd(RWKV_ROWS, b)
    tb = c * math.gcd(RWKV_SUBS, t // c)
    row = lambda w: _full((1, w))
    return pl.pallas_call(
        functools.partial(_rwkv_kernel, c=c, levels=levels, rows=rows),
        grid=(b // rows, t // tb),
        in_specs=[pl.BlockSpec((rows, tb, RW_SHIFT), lambda i, j: (i, j, 0)),
                  pl.BlockSpec((rows, 1, RW_SHIFT), lambda i, j: (i, 0, 0)),
                  pl.BlockSpec((rows, BRANCH, BRANCH), lambda i, j: (i, 0, 0)),
                  row(RW_SHIFT), _full((2 * RW_LORA, 2 * BRANCH)), row(BRANCH), row(BRANCH), row(BRANCH),
                  row(BRANCH), row(BRANCH), row(BRANCH), row(BRANCH)],
        out_specs=[pl.BlockSpec((rows, tb, BRANCH), lambda i, j: (i, j, 0)),
                   pl.BlockSpec((rows, BRANCH, BRANCH), lambda i, j: (i, 0, 0))],
        out_shape=[jax.ShapeDtypeStruct((b, t, BRANCH), F32), jax.ShapeDtypeStruct((b, BRANCH, BRANCH), F32)],
        scratch_shapes=[pltpu.VMEM((rows, BRANCH, BRANCH), F32), pltpu.VMEM((rows, 8, RW_SHIFT), F32)],
        compiler_params=_params(("parallel", "arbitrary")),
        name="rwkv",
    )(za, shift_prev, zbd0, lp["rw_mu"], lp["rw_lora"], lp["rw_w0"], lp["rw_a0"], lp["rw_kk"], lp["rw_ka"],
      lp["rw_rk"], lp["rw_gn_g"], lp["rw_gn_b"])


def _rope_apply(x, cos, s1, s2):
    half = ROPE // 2
    return x * cos + pltpu.roll(x, LANE - half, 1) * s1 + pltpu.roll(x, half, 1) * s2


def _mla_q_kernel(zb_ref, cq_ref, s1q_ref, s2q_ref, ck_ref, s1k_ref, s2k_ref, gqa_ref, wuq_ref, gqn_ref,
                  gkva_ref, gkr_ref, q_ref, ckv_ref, kr_ref):
    z = zb_ref[0]
    zq = z[:, 0:Q_PAD]
    qa = zq * lax.rsqrt(jnp.sum(zq * zq, axis=-1, keepdims=True) * (1.0 / Q_RANK) + NORM_EPS) * gqa_ref[...]
    q = jnp.dot(qa.astype(BF16), wuq_ref[...], preferred_element_type=F32)
    lane = lax.broadcasted_iota(jnp.int32, (1, HEAD_PAD), 1)
    is_nope = lane < NOPE
    sm_scale = math.log2(math.e) / math.sqrt(QK_HD)
    for h in range(HEADS):
        qh = q[:, h * HEAD_PAD:(h + 1) * HEAD_PAD]
        sq = qh * qh
        sn = jnp.sum(jnp.where(is_nope, sq, 0.0), axis=-1, keepdims=True) * (1.0 / NOPE)
        sq_rope = jnp.where(is_nope, 0.0, jnp.where(lane < QK_HD, sq, 0.0))
        sr = jnp.sum(sq_rope, axis=-1, keepdims=True) * (1.0 / ROPE)
        scale = jnp.where(is_nope, lax.rsqrt(sn + NORM_EPS), lax.rsqrt(sr + NORM_EPS))
        xr = qh * scale * gqn_ref[...]
        out = _rope_apply(xr, cq_ref[...], s1q_ref[...], s2q_ref[...])
        q_ref[0, :, h * HEAD_PAD:(h + 1) * HEAD_PAD] = (out * sm_scale).astype(BF16)
    zkv = z[:, Q_PAD:Q_PAD + KV_RANK]
    ckv_ref[0] = zkv * lax.rsqrt(jnp.mean(zkv * zkv, axis=-1, keepdims=True) + NORM_EPS) * gkva_ref[...]
    zkr = z[:, Q_PAD + KV_RANK:]
    xr = zkr * lax.rsqrt(jnp.sum(zkr * zkr, axis=-1, keepdims=True) * (1.0 / ROPE) + NORM_EPS) * gkr_ref[...]
    kr_ref[0] = _rope_apply(xr, ck_ref[...], s1k_ref[...], s2k_ref[...])


def _mla_q(zb, tabs, lp):
    b, t, _ = zb.shape
    tm = min(512, t)
    tab = pl.BlockSpec((tm, LANE), lambda i, j: (j, 0))
    row = lambda w: _full((1, w))
    return pl.pallas_call(
        _mla_q_kernel,
        grid=(b, t // tm),
        in_specs=[pl.BlockSpec((1, tm, 512), lambda i, j: (i, j, 0)), tab, tab, tab, tab, tab, tab,
                  row(Q_PAD), _full((Q_PAD, HEADS * HEAD_PAD)), row(HEAD_PAD), row(KV_RANK), row(LANE)],
        out_specs=[pl.BlockSpec((1, tm, HEADS * HEAD_PAD), lambda i, j: (i, j, 0)),
                   pl.BlockSpec((1, tm, KV_RANK), lambda i, j: (i, j, 0)),
                   pl.BlockSpec((1, tm, LANE), lambda i, j: (i, j, 0))],
        out_shape=[jax.ShapeDtypeStruct((b, t, HEADS * HEAD_PAD), BF16),
                   jax.ShapeDtypeStruct((b, t, KV_RANK), F32),
                   jax.ShapeDtypeStruct((b, t, LANE), F32)],
        compiler_params=_params(("parallel", "parallel")),
        name="mla_q",
    )(zb, *tabs, lp["mla_qa_g"], lp["mla_w_uq"], lp["mla_qn_g"], lp["mla_kva_g"], lp["mla_kr_g"])


def _mla_kv_kernel(ckv_ref, kr_ref, wuk_ref, gkn_ref, wuv_ref, k_ref, v_ref, *, transposed):
    cb = ckv_ref[0].astype(BF16)
    kn = jnp.dot(cb, wuk_ref[...], preferred_element_type=F32)
    krs = pltpu.roll(kr_ref[0], NOPE, 1)
    for h in range(HEADS):
        kh = kn[:, h * HEAD_PAD:(h + 1) * HEAD_PAD]
        ms = jnp.sum(kh * kh, axis=-1, keepdims=True) * (1.0 / NOPE)
        k_ref[0, :, h * HEAD_PAD:(h + 1) * HEAD_PAD] = (kh * lax.rsqrt(ms + NORM_EPS) * gkn_ref[...] + krs).astype(BF16)
    if transposed:
        v = lax.dot_general(wuv_ref[...], cb, (((1,), (1,)), ((), ())), preferred_element_type=F32)
        ones = jnp.ones((VT_ONES, v.shape[1]), F32)
        v = jnp.concatenate([x for h in range(HEADS) for x in (v[h * HD:(h + 1) * HD], ones)], axis=0)
    else:
        v = jnp.dot(cb, wuv_ref[...], preferred_element_type=F32)
    v_ref[0] = v.astype(BF16)


def _mla_kv(ckv, kr, lp, *, transposed):
    b, n, _ = ckv.shape
    tm = 512 if n % 512 == 0 else n
    wide = HEADS * HEAD_PAD
    if transposed:
        wuv, v_spec = lp["mla_w_uv_t"], pl.BlockSpec((1, HEADS * VT_ROWS, tm), lambda i, j: (i, 0, j))
        v_shape = jax.ShapeDtypeStruct((b, HEADS * VT_ROWS, n), BF16)
    else:
        wuv, v_spec = lp["mla_w_uv"], pl.BlockSpec((1, tm, wide), lambda i, j: (i, j, 0))
        v_shape = jax.ShapeDtypeStruct((b, n, wide), BF16)
    return pl.pallas_call(
        functools.partial(_mla_kv_kernel, transposed=transposed),
        grid=(b, n // tm),
        in_specs=[pl.BlockSpec((1, tm, KV_RANK), lambda i, j: (i, j, 0)),
                  pl.BlockSpec((1, tm, LANE), lambda i, j: (i, j, 0)),
                  _full((KV_RANK, wide)), _full((1, HEAD_PAD)), _full(wuv.shape)],
        out_specs=[pl.BlockSpec((1, tm, wide), lambda i, j: (i, j, 0)), v_spec],
        out_shape=[jax.ShapeDtypeStruct((b, n, wide), BF16), v_shape],
        compiler_params=_params(("parallel", "parallel")),
        name="mla_kv",
    )(ckv, kr, lp["mla_w_uk"], lp["mla_kn_g"], wuv)


VT_ONES = 16
VT_ROWS = HD + VT_ONES
FLASH_AHEAD = 2
FLASH_BEHIND = 1


def _col_reduce(x, op, final):
    n = x.shape[0]
    step = n & -n
    parts = [x[i:i + step] for i in range(0, n, step)]
    while len(parts) > 1:
        parts = [op(parts[i], parts[i + 1]) if i + 1 < len(parts) else parts[i] for i in range(0, len(parts), 2)]
    x = parts[0]
    while x.shape[0] > 8:
        half = x.shape[0] // 2
        x = op(x[:half], x[half:])
    return final(x, axis=0, keepdims=True)


def _flash_t_kernel(qt_ref, kt_ref, q_ref, k_ref, vt_ref, o_ref, m_scr, acc_scr, *, tile, qb):
    p_id = pl.program_id(1)
    qi = qt_ref[p_id]
    ki = kt_ref[p_id]
    nrb = tile // qb
    nt = (((1,), (1,)), ((), ()))

    @pl.when(ki == 0)
    def _():
        m_scr[...] = jnp.full(m_scr.shape, -1e30, F32)
        acc_scr[...] = jnp.zeros(acc_scr.shape, F32)

    def block(h, rb, n_keys, masked_tail):
        sl = slice(h * HEAD_PAD, (h + 1) * HEAD_PAD)
        qh = q_ref[0, rb * qb:(rb + 1) * qb, sl]
        s = lax.dot_general(k_ref[0, 0:n_keys, sl], qh, nt, preferred_element_type=F32)
        if masked_tail:
            kr = lax.broadcasted_iota(jnp.int32, (qb, qb), 0)
            qc = lax.broadcasted_iota(jnp.int32, (qb, qb), 1)
            tail = jnp.where((kr // CHUNK) <= (qc // CHUNK), s[n_keys - qb:], -1e30)
            s = tail if n_keys == qb else jnp.concatenate([s[:n_keys - qb], tail], axis=0)
        yield
        idx = h * nrb + rb
        m_prev = m_scr[idx]
        m_new = jnp.maximum(m_prev, _col_reduce(s, jnp.maximum, jnp.max))
        alpha = jnp.exp2(m_prev - m_new)
        m_scr[idx] = m_new
        pb = jnp.exp2(s - m_new).astype(BF16)
        yield
        pv = jnp.dot(vt_ref[0, h * VT_ROWS:(h + 1) * VT_ROWS, 0:n_keys], pb, preferred_element_type=F32)
        acc_scr[idx] = alpha * acc_scr[idx] + pv

    def run_staggered(gens):
        n = len(gens)
        for j in range(n + FLASH_AHEAD + FLASH_BEHIND):
            for lag in (0, FLASH_AHEAD, FLASH_AHEAD + FLASH_BEHIND):
                if 0 <= j - lag < n:
                    next(gens[j - lag], None)

    @pl.when(ki < qi)
    def _():
        run_staggered([block(h, rb, tile, False) for rb in range(nrb) for h in range(HEADS)])

    @pl.when(ki == qi)
    def _():
        run_staggered([block(h, rb, (rb + 1) * qb, True) for rb in range(nrb) for h in range(HEADS)])
        for rb in range(nrb):
            o_t = jnp.concatenate([acc_scr[h * nrb + rb, 0:HD] / acc_scr[h * nrb + rb, HD:HD + 1]
                                   for h in range(HEADS)], axis=0)
            o_ref[0, rb * qb:(rb + 1) * qb, :] = o_t.T


def _flash_t(q, k, vt):
    b, t, wide = q.shape
    tile = min(1024, t)
    qb = min(256, tile)
    nrb = tile // qb
    pairs = [(i, j) for i in range(t // tile) for j in range(i + 1)]
    qt = jnp.asarray(np.array([p[0] for p in pairs], np.int32))
    kt = jnp.asarray(np.array([p[1] for p in pairs], np.int32))
    grid_spec = pltpu.PrefetchScalarGridSpec(
        num_scalar_prefetch=2,
        grid=(b, len(pairs)),
        in_specs=[pl.BlockSpec((1, tile, wide), lambda i, p, qt, kt: (i, qt[p], 0)),
                  pl.BlockSpec((1, tile, wide), lambda i, p, qt, kt: (i, kt[p], 0)),
                  pl.BlockSpec((1, HEADS * VT_ROWS, tile), lambda i, p, qt, kt: (i, 0, kt[p]))],
        out_specs=pl.BlockSpec((1, tile, BRANCH), lambda i, p, qt, kt: (i, qt[p], 0)),
        scratch_shapes=[pltpu.VMEM((HEADS * nrb, 1, qb), F32), pltpu.VMEM((HEADS * nrb, VT_ROWS, qb), F32)],
    )
    return pl.pallas_call(
        functools.partial(_flash_t_kernel, tile=tile, qb=qb),
        grid_spec=grid_spec,
        out_shape=jax.ShapeDtypeStruct((b, t, BRANCH), F32),
        compiler_params=_params(("parallel", "arbitrary")),
        name="flash_t",
    )(qt, kt, q, k, vt)


def _attend_cache_kernel(q_ref, k_ref, v_ref, o_ref, *, kv_valid):
    tq, tk = q_ref.shape[1], k_ref.shape[1]
    valid = lax.broadcasted_iota(jnp.int32, (tq, tk), 1) < kv_valid
    outs = []
    for h in range(HEADS):
        sl = slice(h * HEAD_PAD, (h + 1) * HEAD_PAD)
        s = lax.dot_general(q_ref[0, :, sl], k_ref[0, :, sl], (((1,), (1,)), ((), ())), preferred_element_type=F32)
        s = jnp.where(valid, s, -1e30)
        p = jnp.exp2(s - jnp.max(s, axis=-1, keepdims=True))
        pv = jnp.dot(p.astype(BF16), v_ref[0, :, sl], preferred_element_type=F32)
        outs.append(pv / jnp.sum(p, axis=-1, keepdims=True))
    o_ref[0, :, 0:LANE] = outs[0] + pltpu.roll(outs[1], HD, 1)
    o_ref[0, :, LANE:2 * LANE] = outs[2] + pltpu.roll(outs[3], HD, 1)


def _attend_cache(q, k, v, *, kv_valid):
    b, t, wide = q.shape
    n = k.shape[1]
    assert (t - 1) // CHUNK == 0 and (kv_valid - t) % CHUNK == 0, "queries must sit in the last key chunk"
    return pl.pallas_call(
        functools.partial(_attend_cache_kernel, kv_valid=kv_valid),
        grid=(b,),
        in_specs=[pl.BlockSpec((1, t, wide), lambda i: (i, 0, 0)), pl.BlockSpec((1, n, wide), lambda i: (i, 0, 0)),
                  pl.BlockSpec((1, n, wide), lambda i: (i, 0, 0))],
        out_specs=pl.BlockSpec((1, t, BRANCH), lambda i: (i, 0, 0)),
        out_shape=jax.ShapeDtypeStruct((b, t, BRANCH), F32),
        compiler_params=_params(("parallel",)),
        name="attend_cache",
    )(q, k, v)


def _combine_kernel(x_ref, ya_ref, yb_ref, zc_ref, zd_ref, halo_ref, zg_ref, lng_ref, lnb_ref, ws_ref, sb_ref,
                    pw_ref, psc_ref, wout_ref, *out_refs, tm, cs, n_hist, halo_is_prev_tile):
    out_ref = out_refs[0]
    ti = pl.program_id(1)
    lane = lax.broadcasted_iota(jnp.int32, (1, BRANCH), 1)

    zc = zc_ref[0]
    u = zc[:, 0:BRANCH]
    vraw = zc[:, BRANCH:]
    xc = vraw - jnp.mean(vraw, axis=-1, keepdims=True)
    var = jnp.mean(xc * xc, axis=-1, keepdims=True)
    vn = xc * lax.rsqrt(var + SGU_EPS) * lng_ref[...] + lnb_ref[...]
    if len(out_refs) > 1:
        out_refs[1][0] = vn
    vnb = vn.astype(BF16)
    zero = jnp.zeros((), BF16)
    parts = []
    for n in range(tm // cs):
        vch = vnb[n * cs:(n + 1) * cs]
        stacked = jnp.concatenate([jnp.where(lane // HD == h, vch, zero) for h in range(HEADS)], axis=0)
        parts.append(jnp.dot(ws_ref[...], stacked, preferred_element_type=F32) + sb_ref[...])
    yc = u * (parts[0] if len(parts) == 1 else jnp.concatenate(parts, axis=0))

    zd = zd_ref[0]
    halo = halo_ref[0]
    if halo_is_prev_tile:
        halo = jnp.where(ti == 0, 0.0, halo)
    ext = jnp.concatenate([halo, zd], axis=0)
    sums = []
    acc = ext
    for sh in (1, 2, 4, 8):
        acc = acc + pltpu.roll(acc, sh, 0)
        sums.append(acc[HALO:])
    t_idx = lax.broadcasted_iota(jnp.int32, (tm, BRANCH), 0) + (ti * tm + n_hist + 1)
    lane2 = lax.broadcasted_iota(jnp.int32, (tm, BRANCH), 1)
    win = jnp.where(lane2 < HD, POOL_WINDOWS[0],
                    jnp.where(lane2 < 2 * HD, POOL_WINDOWS[1],
                              jnp.where(lane2 < 3 * HD, POOL_WINDOWS[2], POOL_WINDOWS[3])))
    cnt = jnp.minimum(t_idx, win).astype(F32)
    ssum = jnp.where(lane2 < HD, sums[0],
                     jnp.where(lane2 < 2 * HD, sums[1], jnp.where(lane2 < 3 * HD, sums[2], sums[3])))
    d = ssum / cnt - zd
    yd = jnp.dot(d.astype(BF16), pw_ref[...], preferred_element_type=F32) * psc_ref[...]

    g = zg_ref[0]
    gate = g / (1.0 + jnp.exp(-g))
    y = jnp.concatenate([ya_ref[0], yb_ref[0], yc, yd], axis=-1) * gate
    out_ref[0] = x_ref[0] + jnp.dot(y.astype(BF16), wout_ref[...], preferred_element_type=F32)


def _combine(x, ya, yb, zc, zd, halo, zg, lp, *, n_hist, want_vn):
    b, t, _ = x.shape
    tm = min(512, t)
    cs = min(SGU_CHUNK, t)
    halo_is_prev_tile = halo is None
    if halo_is_prev_tile:
        halo = zd
        per = tm // HALO
        halo_spec = pl.BlockSpec((1, HALO, BRANCH), lambda i, j: (i, jnp.maximum(j * per - 1, 0), 0))
    else:
        halo_spec = pl.BlockSpec((1, HALO, BRANCH), lambda i, j: (i, 0, 0))
    blk = lambda w: pl.BlockSpec((1, tm, w), lambda i, j: (i, j, 0))
    row = lambda w: _full((1, w))
    out_specs = [blk(D_MODEL)]
    out_shape = [jax.ShapeDtypeStruct((b, t, D_MODEL), F32)]
    if want_vn:
        out_specs.append(blk(BRANCH))
        out_shape.append(jax.ShapeDtypeStruct((b, t, BRANCH), F32))
    res = pl.pallas_call(
        functools.partial(_combine_kernel, tm=tm, cs=cs, n_hist=n_hist, halo_is_prev_tile=halo_is_prev_tile),
        grid=(b, t // tm),
        in_specs=[blk(D_MODEL), blk(BRANCH), blk(BRANCH), blk(2 * BRANCH), blk(BRANCH), halo_spec, blk(D_MODEL),
                  row(BRANCH), row(BRANCH), _full((cs, HEADS * cs)), _full((cs, BRANCH)),
                  _full((BRANCH, BRANCH)), row(BRANCH), _full((D_MODEL, D_MODEL))],
        out_specs=out_specs,
        out_shape=out_shape,
        compiler_params=_params(("parallel", "parallel")),
        name="combine",
    )(x, ya, yb, zc, zd, halo, zg, lp["sgu_ln_g"], lp["sgu_ln_b"], lp["sgu_w_cat"][cs], lp["sgu_b_tab"][cs],
      lp["pool_w_bd"], lp["pool_scale"], lp["w_out"])
    return res


def _pad_cols(w, total):
    return jnp.pad(w, ((0, 0), (0, total - w.shape[1])))


def _head_slots(w, real):
    rows = w.shape[0]
    w = w.reshape(rows, HEADS, real)
    return jnp.pad(w, ((0, 0), (0, 0), (0, HEAD_PAD - real))).reshape(rows, HEADS * HEAD_PAD)


def _layer_params(l, norm_g, w_in, w_out, rw_mu, rw_w0, rw_w2, rw_a0, rw_a2, rw_kk, rw_ka, rw_rk, rw_gn_g,
                  rw_gn_b, mla_qa_g, mla_w_uq, mla_kva_g, mla_w_uk, mla_w_uv, mla_q_norm_g, mla_k_norm_g, sgu_w,
                  sgu_b, sgu_ln_g, sgu_ln_b, pool_w, pool_scale, sgu_sizes):
    wi = w_in[l]
    off_b = RW_SHIFT
    off_c = off_b + Q_RANK + KV_RANK + ROPE
    off_d = off_c + 2 * BRANCH
    off_g = off_d + BRANCH
    w_in_r = jnp.concatenate([
        wi[:, :off_b],
        _pad_cols(wi[:, off_b:off_b + Q_RANK], Q_PAD),
        wi[:, off_b + Q_RANK:off_b + Q_RANK + KV_RANK],
        _pad_cols(wi[:, off_b + Q_RANK + KV_RANK:off_c], LANE),
        wi[:, off_c:off_d], wi[:, off_d:off_g], wi[:, off_g:]], axis=1).astype(BF16)
    zeros = jnp.zeros((RW_LORA, BRANCH), F32)
    rw_lora = jnp.concatenate([jnp.concatenate([rw_w2[l], zeros], 1), jnp.concatenate([zeros, rw_a2[l]], 1)], 0)
    row = lambda v: v.reshape(1, -1).astype(F32)
    qn = mla_q_norm_g[l]
    kn = mla_k_norm_g[l]
    tri = jnp.tril(jnp.ones((SGU_CHUNK, SGU_CHUNK), bool))
    ws = jnp.where(tri[None], sgu_w[l], 0.0)
    sgu_w_cat = {cs: jnp.concatenate([ws[h, :cs, :cs] for h in range(HEADS)], axis=1).astype(BF16)
                 for cs in sgu_sizes}
    sgu_b_tab = {cs: jnp.repeat(jnp.transpose(sgu_b[l])[:cs], HD, axis=1) for cs in sgu_sizes}
    eye = jnp.eye(HEADS, dtype=F32)
    pool_w_bd = jnp.einsum("gcd,gh->gchd", pool_w[l], eye).reshape(BRANCH, BRANCH).astype(BF16)
    return {
        "norm_g": row(norm_g[l]), "w_in": w_in_r, "w_out": w_out[l].astype(BF16),
        "rw_mu": row(rw_mu[l]), "rw_lora": rw_lora.astype(BF16), "rw_w0": row(rw_w0[l]), "rw_a0": row(rw_a0[l]),
        "rw_kk": row(rw_kk[l]), "rw_ka": row(rw_ka[l]), "rw_rk": row(rw_rk[l]), "rw_gn_g": row(rw_gn_g[l]),
        "rw_gn_b": row(rw_gn_b[l]),
        "mla_qa_g": _pad_cols(row(mla_qa_g[l]), Q_PAD),
        "mla_w_uq": jnp.pad(_head_slots(mla_w_uq[l], QK_HD), ((0, Q_PAD - Q_RANK), (0, 0))).astype(BF16),
        "mla_qn_g": _pad_cols(row(qn), HEAD_PAD),
        "mla_kva_g": row(mla_kva_g[l]),
        "mla_kr_g": _pad_cols(row(kn[NOPE:]), LANE),
        "mla_w_uk": _head_slots(mla_w_uk[l], NOPE).astype(BF16),
        "mla_kn_g": _pad_cols(row(kn[:NOPE]), HEAD_PAD),
        "mla_w_uv": _head_slots(mla_w_uv[l], HD).astype(BF16),
        "mla_w_uv_t": jnp.transpose(mla_w_uv[l]).astype(BF16),
        "sgu_ln_g": row(sgu_ln_g[l]), "sgu_ln_b": row(sgu_ln_b[l]), "sgu_w_cat": sgu_w_cat, "sgu_b_tab": sgu_b_tab,
        "pool_w_bd": pool_w_bd, "pool_scale": row(pool_scale[l]),
    }


def _rope_tables(pos):
    half = ROPE // 2
    inv = ROPE_THETA ** (-jnp.arange(half, dtype=F32) / half)
    ang = pos.astype(F32)[:, None] * inv[None, :]
    cos, sin = jnp.cos(ang), jnp.sin(ang)
    t = pos.shape[0]

    def place(first, second, off, fill):
        base = jnp.full((t, LANE), fill, F32)
        base = lax.dynamic_update_slice(base, first, (0, off))
        return lax.dynamic_update_slice(base, second, (0, off + half))

    z = jnp.zeros_like(sin)
    tabs = []
    for off, fill in ((NOPE, 1.0), (0, 0.0)):
        tabs += [place(cos, cos, off, fill), place(-sin, z, off, 0.0), place(z, sin, off, 0.0)]
    return tabs


def _state_to_blockdiag(s):
    b = s.shape[0]
    eye = jnp.eye(HEADS, dtype=s.dtype)
    return jnp.einsum("bhij,hg->bhjgi", s, eye).reshape(b, BRANCH, BRANCH)


def _blockdiag_to_state(z):
    b = z.shape[0]
    z5 = z.reshape(b, HEADS, HD, HEADS, HD)
    return jnp.stack([jnp.swapaxes(z5[:, h, :, h, :], 1, 2) for h in range(HEADS)], axis=1)


def _layer(x, lp, tabs, hist):
    b, t, _ = x.shape
    za, zb, zc, zd, zg = _inproj(x.reshape(b * t, D_MODEL), lp["norm_g"], lp["w_in"])
    za = za.reshape(b, t, RW_SHIFT)
    zb = zb.reshape(b, t, 512)
    zc = zc.reshape(b, t, 2 * BRANCH)
    zd = zd.reshape(b, t, BRANCH)
    zg = zg.reshape(b, t, D_MODEL)
    if hist is None:
        shift_prev = jnp.zeros((b, 1, RW_SHIFT), F32)
        zbd0 = jnp.zeros((b, BRANCH, BRANCH), F32)
    else:
        c_ckv, c_kr, s_wkv, s_shift, s_pool = hist
        shift_prev = s_shift[:, None, :]
        zbd0 = _state_to_blockdiag(s_wkv)
    ya, zfin = _rwkv(za, shift_prev, zbd0, lp)
    q, ckv, krp = _mla_q(zb, tabs, lp)
    if hist is None:
        k, vt = _mla_kv(ckv, krp, lp, transposed=True)
        yb = _flash_t(q, k, vt)
        halo, n_hist = None, 0
        pool_src = zd
    else:
        past = c_ckv.shape[1]
        n_all = past + t
        n_pad = -(-n_all // LANE) * LANE
        ckv_all = jnp.pad(jnp.concatenate([c_ckv, ckv], axis=1), ((0, 0), (0, n_pad - n_all), (0, 0)))
        kr_all = jnp.concatenate([jnp.pad(c_kr, ((0, 0), (0, 0), (0, LANE - ROPE))), krp], axis=1)
        kr_all = jnp.pad(kr_all, ((0, 0), (0, n_pad - n_all), (0, 0)))
        k, v = _mla_kv(ckv_all, kr_all, lp, transposed=False)
        yb = _attend_cache(q, k, v, kv_valid=n_all)
        halo = jnp.pad(s_pool, ((0, 0), (HALO - POOL_HIST, 0), (0, 0)))
        n_hist = POOL_HIST
        pool_src = jnp.concatenate([s_pool, zd], axis=1)
    res = _combine(x, ya, yb, zc, zd, halo, zg, lp, n_hist=n_hist, want_vn=hist is not None)
    state = (ckv, krp[..., :ROPE], _blockdiag_to_state(zfin), za[:, -1], pool_src[:, pool_src.shape[1] - POOL_HIST:])
    if hist is None:
        return res[0], state
    return res[0], state + (res[1],)


def kernel(x_prompt, x_sample, cache_ckv, cache_krope, state_wkv, state_shift, state_pool, norm_g, w_in, w_out, rw_mu, rw_w0, rw_w2, rw_a0, rw_a2, rw_kk, rw_ka, rw_rk, rw_gn_g, rw_gn_b, mla_qa_g, mla_w_uq, mla_kva_g, mla_w_uk, mla_w_uv, mla_q_norm_g, mla_k_norm_g, sgu_w, sgu_b, sgu_ln_g, sgu_ln_b, pool_w, pool_scale):
    depth = w_in.shape[0]
    t_p = x_prompt.shape[1]
    t_s = x_sample.shape[1]
    past = cache_ckv.shape[2]
    sgu_sizes = sorted({min(SGU_CHUNK, t_p), min(SGU_CHUNK, t_s)})
    tabs_p = _rope_tables(jnp.arange(t_p))
    tabs_s = _rope_tables(past + jnp.arange(t_s))
    yp, ys = x_prompt, x_sample
    new_p = [[] for _ in range(5)]
    new_s = [[] for _ in range(6)]
    for l in range(depth):
        lp = _layer_params(l, norm_g, w_in, w_out, rw_mu, rw_w0, rw_w2, rw_a0, rw_a2, rw_kk, rw_ka, rw_rk, rw_gn_g,
                           rw_gn_b, mla_qa_g, mla_w_uq, mla_kva_g, mla_w_uk, mla_w_uv, mla_q_norm_g, mla_k_norm_g,
                           sgu_w, sgu_b, sgu_ln_g, sgu_ln_b, pool_w, pool_scale, sgu_sizes)
        yp, st_p = _layer(yp, lp, tabs_p, None)
        ys, st_s = _layer(ys, lp, tabs_s, (cache_ckv[l], cache_krope[l], state_wkv[l], state_shift[l], state_pool[l]))
        for lst, arr in zip(new_p, st_p):
            lst.append(arr)
        for lst, arr in zip(new_s, st_s):
            lst.append(arr)
    outs_p = [jnp.stack(a, axis=0) for a in new_p]
    outs_s = [jnp.stack(a, axis=0) for a in new_s]
    return (yp, ys, *outs_p, *outs_s)
```

```python
import functools
import math

import jax
import jax.numpy as jnp
import numpy as np
from jax import lax
from jax.experimental import pallas as pl
from jax.experimental.pallas import tpu as pltpu

F32 = jnp.float32
BF16 = jnp.bfloat16

D_MODEL = 1024
CHUNK = 64
BRANCH = 256
NORM_EPS = 1e-6
HEADS = 4
HD = 64
RW_LORA = 64
RW_SHIFT = 3 * BRANCH + 2 * RW_LORA
RW_GN_EPS = 64e-5
NOPE = 64
ROPE = 32
QK_HD = NOPE + ROPE
Q_RANK = 192
KV_RANK = 128
ROPE_THETA = 10000.0
SGU_CHUNK = 128
SGU_EPS = 1e-5
POOL_WINDOWS = (2, 4, 8, 16)
POOL_HIST = 15
D_IN = 3040

LANE = 128
HALO = 16
Q_PAD = 256
HEAD_PAD = 128
VMEM_LIMIT = 48 * 1024 * 1024

SEG_A = (0, 896)
SEG_B = (896, 1408)
SEG_C = (1408, 1920)
SEG_D = (1920, 2176)
SEG_G = (2176, 3200)
D_IN_PAD = 3200


def _params(sem):
    return pltpu.CompilerParams(dimension_semantics=sem, vmem_limit_bytes=VMEM_LIMIT)


def _full(shape):
    nd = len(shape)
    return pl.BlockSpec(shape, lambda *_: (0,) * nd)


def _inproj_kernel(x_ref, g_ref, w_ref, za_ref, zb_ref, zc_ref, zd_ref, zg_ref):
    x = x_ref[...]
    ms = jnp.mean(x * x, axis=-1, keepdims=True)
    xn = (x * lax.rsqrt(ms + NORM_EPS) * g_ref[...]).astype(BF16)
    for ref, (lo, hi) in ((za_ref, SEG_A), (zb_ref, SEG_B), (zc_ref, SEG_C), (zd_ref, SEG_D), (zg_ref, SEG_G)):
        ref[...] = jnp.dot(xn, w_ref[:, lo:hi], preferred_element_type=F32)


def _inproj(x2d, g, w):
    n = x2d.shape[0]
    tm = min(512, n)
    widths = [hi - lo for lo, hi in (SEG_A, SEG_B, SEG_C, SEG_D, SEG_G)]
    return pl.pallas_call(
        _inproj_kernel,
        grid=(n // tm,),
        in_specs=[pl.BlockSpec((tm, D_MODEL), lambda i: (i, 0)), _full((1, D_MODEL)), _full((D_MODEL, D_IN_PAD))],
        out_specs=[pl.BlockSpec((tm, wd), lambda i: (i, 0)) for wd in widths],
        out_shape=[jax.ShapeDtypeStruct((n, wd), F32) for wd in widths],
        compiler_params=_params(("parallel",)),
        name="inproj",
    )(x2d, g, w)


def _softplus(x):
    return jnp.maximum(x, 0.0) + jnp.log(1.0 + jnp.exp(-jnp.abs(x)))


def _split_bf16(x, parts):
    out = []
    for _ in range(parts - 1):
        hi = x.astype(BF16)
        out.append(hi)
        x = x - hi.astype(F32)
    out.append(x.astype(BF16))
    return out


def _head_sum(x, ones_bd):
    return sum(jnp.dot(p, ones_bd, preferred_element_type=F32) for p in _split_bf16(x, 2))


def _dot_s(a, b):
    return jnp.dot(a.astype(BF16), b.astype(BF16), preferred_element_type=F32)


def _dot_nt_s(a, b):
    return lax.dot_general(a.astype(BF16), b.astype(BF16), (((1,), (1,)), ((), ())), preferred_element_type=F32)


def _rwkv_kernel(za_ref, shift_ref, z0_ref, mu_ref, wl_ref, w0_ref, a0_ref, kkg_ref, kag_ref, rkg_ref,
                 gng_ref, gnb_ref, ya_ref, zfin_ref, z_scr, carry_scr, *, c, levels, rows):
    ci = pl.program_id(1)

    @pl.when(ci == 0)
    def _():
        z_scr[...] = z0_ref[...]
        for i in range(rows):
            carry_scr[i] = jnp.broadcast_to(shift_ref[i], carry_scr.shape[1:])

    subs = za_ref.shape[1] // c
    states = {}
    consts = _rwkv_consts(c)
    gens, starts = [], []
    for s in range(subs):
        for i in range(rows):
            za = za_ref[i, s * c:(s + 1) * c, :]
            carry = carry_scr[i, 0:1, :] if s == 0 else za_ref[i, s * c - 1:s * c, :]
            get_z = (lambda i=i: z_scr[i]) if s == 0 else (lambda i=i, s=s: states[(i, s - 1)])
            publish = lambda z_new, i=i, s=s: states.__setitem__((i, s), z_new)
            gens.append(_rwkv_chunk(za, carry, get_z, publish, consts, mu_ref, wl_ref, w0_ref, a0_ref, kkg_ref, kag_ref,
                                    rkg_ref, gng_ref, gnb_ref, c=c, levels=levels))
            starts.append(s * RWKV_SKEW)
    yas = _run_lockstep(gens, starts)
    ya_ref[...] = jnp.stack([jnp.concatenate([yas[s * rows + i] for s in range(subs)], axis=0)
                             for i in range(rows)], axis=0)
    z_scr[...] = jnp.stack([states[(i, subs - 1)] for i in range(rows)], axis=0)
    carry_scr[...] = jnp.stack([jnp.broadcast_to(za_ref[i, subs * c - 1:subs * c, :], carry_scr.shape[1:])
                                for i in range(rows)], axis=0)

    @pl.when(ci == pl.num_programs(1) - 1)
    def _():
        zfin_ref[...] = z_scr[...]


def _rwkv_consts(c):
    hc = HEADS * c
    lane = lax.broadcasted_iota(jnp.int32, (c, BRANCH), 1)
    rr = lax.broadcasted_iota(jnp.int32, (BRANCH, BRANCH), 0)
    cc = lax.broadcasted_iota(jnp.int32, (BRANCH, BRANCH), 1)
    tr = lax.broadcasted_iota(jnp.int32, (c, c), 0)
    tc = lax.broadcasted_iota(jnp.int32, (c, c), 1)
    wt = lax.broadcasted_iota(jnp.int32, (c, hc), 0)
    wl = lax.broadcasted_iota(jnp.int32, (c, hc), 1)
    ws = wl % c
    bits = lambda cond: jnp.where(cond, 1.0, 0.0).astype(BF16)
    return {
        "head_block": rr // HD == cc // HD,
        "ones_bd": bits(rr // HD == cc // HD),
        "tril": bits(tc <= tr),
        "head_lanes": [bits(lane // HD == h) for h in range(HEADS)],
        "head_cols": [bits(wl // c == h) for h in range(HEADS)],
        "strict": ws < wt, "incl": ws <= wt,
        "eye_w": jnp.where(ws == wt, 1.0, 0.0).astype(F32),
    }


def _rwkv_chunk(za, carry, get_z, publish, consts, mu_ref, wl_ref, w0_ref, a0_ref, kkg_ref, kag_ref, rkg_ref,
                gng_ref, gnb_ref, *, c, levels):
    row = lax.broadcasted_iota(jnp.int32, za.shape, 0)
    prev = jnp.where(row == 0, carry, pltpu.roll(za, 1, 0))
    zs = za + mu_ref[...] * (prev - za)
    r = zs[:, 0:BRANCH]
    k = zs[:, BRANCH:2 * BRANCH]
    v = zs[:, 2 * BRANCH:3 * BRANCH]
    wa = zs[:, 3 * BRANCH:RW_SHIFT]
    lane128 = lax.broadcasted_iota(jnp.int32, wa.shape, 1)
    lora_in = jnp.where(lane128 < RW_LORA, jnp.tanh(wa), wa).astype(BF16)
    lora = jnp.dot(lora_in, wl_ref[...], preferred_element_type=F32)
    w_log = -_softplus(-(w0_ref[...] + lora[:, :BRANCH])) - 0.5
    logd = -jnp.exp(w_log)
    asig = 1.0 / (1.0 + jnp.exp(-(a0_ref[...] + lora[:, BRANCH:])))

    ones_bd = consts["ones_bd"]

    kkv = k * kkg_ref[...]
    kk = kkv * lax.rsqrt(jnp.maximum(_head_sum(kkv * kkv, ones_bd), 1e-24))
    kmod = k * (1.0 + (asig - 1.0) * kag_ref[...])
    avec = -kk
    bvec = kk * asig

    lcum = sum(jnp.dot(consts["tril"], p, preferred_element_type=F32) for p in _split_bf16(logd, 3))
    lexc = lcum - logd
    ltot = lcum[c - 1:c, :]
    g_in = jnp.exp(lcum)
    g_ex = jnp.exp(lexc)
    g_inv = jnp.exp(-lcum)
    g_rem = jnp.exp(ltot - lcum)

    def stack(x):
        xb = x.astype(BF16)
        return jnp.concatenate([xb * m for m in consts["head_lanes"]], axis=0)

    strict = lambda x: jnp.where(consts["strict"], x, 0.0)
    incl = lambda x: jnp.where(consts["incl"], x, 0.0)

    def stack_w(x):
        xb = x.astype(BF16)
        return jnp.concatenate([xb * m for m in consts["head_cols"]], axis=0)

    a_t = avec * g_ex
    r_t = r * g_in
    b_s = stack(bvec * g_inv)
    k_s = stack(kmod * g_inv)
    v_s = stack(v)
    yield
    a_ab = strict(_dot_nt_s(a_t, b_s))
    a_ak = strict(_dot_nt_s(a_t, k_s))
    a_rb = incl(_dot_nt_s(r_t, b_s))
    a_rk = incl(_dot_nt_s(r_t, k_s))
    yield

    t_w = consts["eye_w"] + a_ab
    pw = a_ab
    bd = stack_w(pw)
    for _ in range(levels - 1):
        pw = _dot_s(pw, bd)
        yield
        bd = stack_w(pw)
        t_w = t_w + _dot_s(t_w, bd)

    g_col = jnp.exp(jnp.sum(logd.T, axis=1, keepdims=True))
    pad = [jnp.zeros((max(LANE - 2 * c, 0) // 2, BRANCH), F32)] * 2 if 2 * c < LANE else []
    lhs_t = jnp.concatenate([bvec * g_rem, kmod * g_rem] + pad, axis=0).T
    yield
    z = get_z()
    w_z = _dot_s(a_t, z) + _dot_s(a_ak, v_s)
    y_part = _dot_s(r_t, z) + _dot_s(a_rk, v_s)
    yield
    u = _dot_s(t_w, stack(w_z))
    yield
    y = y_part + _dot_s(a_rb, stack(u))
    upd = _dot_s(lhs_t, jnp.concatenate([u, v] + pad, axis=0))
    z_new = z * g_col + jnp.where(consts["head_block"], upd, 0.0)
    publish(z_new)
    yield

    mean = _head_sum(y, ones_bd) * (1.0 / HD)
    yield
    yc = y - mean
    var = _head_sum(yc * yc, ones_bd) * (1.0 / HD)
    yn = yc * lax.rsqrt(var + RW_GN_EPS) * gng_ref[...] + gnb_ref[...]
    bonus = _head_sum(r * kmod * rkg_ref[...], ones_bd) * v
    return yn + bonus


def _run_lockstep(gens, starts):
    results = [None] * len(gens)
    done = [False] * len(gens)
    tick = 0
    while not all(done):
        for i, g in enumerate(gens):
            if done[i] or tick < starts[i]:
                continue
            try:
                next(g)
            except StopIteration as stop:
                results[i], done[i] = stop.value, True
        tick += 1
    return results


RWKV_SKEW = 3
RWKV_ROWS = 4
RWKV_SUBS = 2


def _rwkv(za, shift_prev, zbd0, lp):
    b, t, _ = za.shape
    c = min(CHUNK, t)
    levels = int(math.log2(c))
    rows = math.gcd(RWKV_ROWS, b)
    tb = c * math.gcd(RWKV_SUBS, t // c)
    row = lambda w: _full((1, w))
    return pl.pallas_call(
        functools.partial(_rwkv_kernel, c=c, levels=levels, rows=rows),
        grid=(b // rows, t // tb),
        in_specs=[pl.BlockSpec((rows, tb, RW_SHIFT), lambda i, j: (i, j, 0)),
                  pl.BlockSpec((rows, 1, RW_SHIFT), lambda i, j: (i, 0, 0)),
                  pl.BlockSpec((rows, BRANCH, BRANCH), lambda i, j: (i, 0, 0)),
                  row(RW_SHIFT), _full((2 * RW_LORA, 2 * BRANCH)), row(BRANCH), row(BRANCH), row(BRANCH),
                  row(BRANCH), row(BRANCH), row(BRANCH), row(BRANCH)],
        out_specs=[pl.BlockSpec((rows, tb, BRANCH), lambda i, j: (i, j, 0)),
                   pl.BlockSpec((rows, BRANCH, BRANCH), lambda i, j: (i, 0, 0))],
        out_shape=[jax.ShapeDtypeStruct((b, t, BRANCH), F32), jax.ShapeDtypeStruct((b, BRANCH, BRANCH), F32)],
        scratch_shapes=[pltpu.VMEM((rows, BRANCH, BRANCH), F32), pltpu.VMEM((rows, 8, RW_SHIFT), F32)],
        compiler_params=_params(("parallel", "arbitrary")),
        name="rwkv",
    )(za, shift_prev, zbd0, lp["rw_mu"], lp["rw_lora"], lp["rw_w0"], lp["rw_a0"], lp["rw_kk"], lp["rw_ka"],
      lp["rw_rk"], lp["rw_gn_g"], lp["rw_gn_b"])


def _rope_apply(x, cos, s1, s2):
    half = ROPE // 2
    return x * cos + pltpu.roll(x, LANE - half, 1) * s1 + pltpu.roll(x, half, 1) * s2


def _mla_q_kernel(zb_ref, cq_ref, s1q_ref, s2q_ref, ck_ref, s1k_ref, s2k_ref, gqa_ref, wuq_ref, gqn_ref,
                  gkva_ref, gkr_ref, q_ref, ckv_ref, kr_ref, kr32_ref):
    z = zb_ref[0]
    zq = z[:, 0:Q_PAD]
    qa = zq * lax.rsqrt(jnp.sum(zq * zq, axis=-1, keepdims=True) * (1.0 / Q_RANK) + NORM_EPS) * gqa_ref[...]
    q = jnp.dot(qa.astype(BF16), wuq_ref[...], preferred_element_type=F32)
    lane = lax.broadcasted_iota(jnp.int32, (1, HEAD_PAD), 1)
    is_nope = lane < NOPE
    sm_scale = math.log2(math.e) / math.sqrt(QK_HD)
    for h in range(HEADS):
        qh = q[:, h * HEAD_PAD:(h + 1) * HEAD_PAD]
        sq = qh * qh
        sn = jnp.sum(jnp.where(is_nope, sq, 0.0), axis=-1, keepdims=True) * (1.0 / NOPE)
        sq_rope = jnp.where(is_nope, 0.0, jnp.where(lane < QK_HD, sq, 0.0))
        sr = jnp.sum(sq_rope, axis=-1, keepdims=True) * (1.0 / ROPE)
        scale = jnp.where(is_nope, lax.rsqrt(sn + NORM_EPS), lax.rsqrt(sr + NORM_EPS))
        xr = qh * scale * gqn_ref[...]
        out = _rope_apply(xr, cq_ref[...], s1q_ref[...], s2q_ref[...])
        q_ref[0, :, h * HEAD_PAD:(h + 1) * HEAD_PAD] = (out * sm_scale).astype(BF16)
    zkv = z[:, Q_PAD:Q_PAD + KV_RANK]
    ckv_ref[0] = zkv * lax.rsqrt(jnp.mean(zkv * zkv, axis=-1, keepdims=True) + NORM_EPS) * gkva_ref[...]
    zkr = z[:, Q_PAD + KV_RANK:]
    xr = zkr * lax.rsqrt(jnp.sum(zkr * zkr, axis=-1, keepdims=True) * (1.0 / ROPE) + NORM_EPS) * gkr_ref[...]
    kr = _rope_apply(xr, ck_ref[...], s1k_ref[...], s2k_ref[...])
    kr_ref[0] = kr
    kr32_ref[0] = kr[:, 0:ROPE]


def _mla_q(zb, tabs, lp):
    b, t, _ = zb.shape
    tm = min(512, t)
    tab = pl.BlockSpec((tm, LANE), lambda i, j: (j, 0))
    row = lambda w: _full((1, w))
    return pl.pallas_call(
        _mla_q_kernel,
        grid=(b, t // tm),
        in_specs=[pl.BlockSpec((1, tm, 512), lambda i, j: (i, j, 0)), tab, tab, tab, tab, tab, tab,
                  row(Q_PAD), _full((Q_PAD, HEADS * HEAD_PAD)), row(HEAD_PAD), row(KV_RANK), row(LANE)],
        out_specs=[pl.BlockSpec((1, tm, HEADS * HEAD_PAD), lambda i, j: (i, j, 0)),
                   pl.BlockSpec((1, tm, KV_RANK), lambda i, j: (i, j, 0)),
                   pl.BlockSpec((1, tm, LANE), lambda i, j: (i, j, 0)),
                   pl.BlockSpec((1, tm, ROPE), lambda i, j: (i, j, 0))],
        out_shape=[jax.ShapeDtypeStruct((b, t, HEADS * HEAD_PAD), BF16),
                   jax.ShapeDtypeStruct((b, t, KV_RANK), F32),
                   jax.ShapeDtypeStruct((b, t, LANE), F32),
                   jax.ShapeDtypeStruct((b, t, ROPE), F32)],
        compiler_params=_params(("parallel", "parallel")),
        name="mla_q",
    )(zb, *tabs, lp["mla_qa_g"], lp["mla_w_uq"], lp["mla_qn_g"], lp["mla_kva_g"], lp["mla_kr_g"])


def _mla_kv_kernel(ckv_ref, kr_ref, wuk_ref, gkn_ref, wuv_ref, k_ref, v_ref, *, transposed):
    cb = ckv_ref[0].astype(BF16)
    kn = jnp.dot(cb, wuk_ref[...], preferred_element_type=F32)
    krs = pltpu.roll(kr_ref[0], NOPE, 1)
    for h in range(HEADS):
        kh = kn[:, h * HEAD_PAD:(h + 1) * HEAD_PAD]
        ms = jnp.sum(kh * kh, axis=-1, keepdims=True) * (1.0 / NOPE)
        k_ref[0, :, h * HEAD_PAD:(h + 1) * HEAD_PAD] = (kh * lax.rsqrt(ms + NORM_EPS) * gkn_ref[...] + krs).astype(BF16)
    if transposed:
        v = lax.dot_general(wuv_ref[...], cb, (((1,), (1,)), ((), ())), preferred_element_type=F32)
        ones = jnp.ones((VT_ONES, v.shape[1]), F32)
        v = jnp.concatenate([x for h in range(HEADS) for x in (v[h * HD:(h + 1) * HD], ones)], axis=0)
    else:
        v = jnp.dot(cb, wuv_ref[...], preferred_element_type=F32)
    v_ref[0] = v.astype(BF16)


def _mla_kv(ckv, kr, lp, *, transposed):
    b, n, _ = ckv.shape
    tm = 512 if n % 512 == 0 else n
    wide = HEADS * HEAD_PAD
    if transposed:
        wuv, v_spec = lp["mla_w_uv_t"], pl.BlockSpec((1, HEADS * VT_ROWS, tm), lambda i, j: (i, 0, j))
        v_shape = jax.ShapeDtypeStruct((b, HEADS * VT_ROWS, n), BF16)
    else:
        wuv, v_spec = lp["mla_w_uv"], pl.BlockSpec((1, tm, wide), lambda i, j: (i, j, 0))
        v_shape = jax.ShapeDtypeStruct((b, n, wide), BF16)
    return pl.pallas_call(
        functools.partial(_mla_kv_kernel, transposed=transposed),
        grid=(b, n // tm),
        in_specs=[pl.BlockSpec((1, tm, KV_RANK), lambda i, j: (i, j, 0)),
                  pl.BlockSpec((1, tm, LANE), lambda i, j: (i, j, 0)),
                  _full((KV_RANK, wide)), _full((1, HEAD_PAD)), _full(wuv.shape)],
        out_specs=[pl.BlockSpec((1, tm, wide), lambda i, j: (i, j, 0)), v_spec],
        out_shape=[jax.ShapeDtypeStruct((b, n, wide), BF16), v_shape],
        compiler_params=_params(("parallel", "parallel")),
        name="mla_kv",
    )(ckv, kr, lp["mla_w_uk"], lp["mla_kn_g"], wuv)


VT_ONES = 16
VT_ROWS = HD + VT_ONES
FLASH_AHEAD = 4
FLASH_BEHIND = 2


def _col_reduce(x, op, final):
    n = x.shape[0]
    step = n & -n
    parts = [x[i:i + step] for i in range(0, n, step)]
    while len(parts) > 1:
        parts = [op(parts[i], parts[i + 1]) if i + 1 < len(parts) else parts[i] for i in range(0, len(parts), 2)]
    x = parts[0]
    while x.shape[0] > 8:
        half = x.shape[0] // 2
        x = op(x[:half], x[half:])
    return final(x, axis=0, keepdims=True)


def _flash_t_kernel(qt_ref, kt_ref, q_ref, k_ref, vt_ref, o_ref, m_scr, acc_scr, *, tile, qb):
    p_id = pl.program_id(1)
    qi = qt_ref[p_id]
    ki = kt_ref[p_id]
    nrb = tile // qb
    nt = (((1,), (1,)), ((), ()))

    @pl.when(ki == 0)
    def _():
        m_scr[...] = jnp.full(m_scr.shape, -1e30, F32)
        acc_scr[...] = jnp.zeros(acc_scr.shape, F32)

    def block(h, rb, kb, masked):
        sl = slice(h * HEAD_PAD, (h + 1) * HEAD_PAD)
        keys = slice(kb * qb, (kb + 1) * qb)
        qh = q_ref[0, rb * qb:(rb + 1) * qb, sl]
        s = lax.dot_general(k_ref[0, keys, sl], qh, nt, preferred_element_type=F32)
        if masked:
            kr = lax.broadcasted_iota(jnp.int32, (qb, qb), 0)
            qc = lax.broadcasted_iota(jnp.int32, (qb, qb), 1)
            s = jnp.where((kr // CHUNK) <= (qc // CHUNK), s, -1e30)
        yield
        idx = h * nrb + rb
        m_prev = m_scr[idx]
        m_new = jnp.maximum(m_prev, _col_reduce(s, jnp.maximum, jnp.max))
        alpha = jnp.exp2(m_prev - m_new)
        m_scr[idx] = m_new
        pb = jnp.exp2(s - m_new).astype(BF16)
        yield
        pv = jnp.dot(vt_ref[0, h * VT_ROWS:(h + 1) * VT_ROWS, keys], pb, preferred_element_type=F32)
        acc_scr[idx] = alpha * acc_scr[idx] + pv

    def run_staggered(gens):
        n = len(gens)
        for j in range(n + FLASH_AHEAD + FLASH_BEHIND):
            for lag in (0, FLASH_AHEAD, FLASH_AHEAD + FLASH_BEHIND):
                if 0 <= j - lag < n:
                    next(gens[j - lag], None)

    @pl.when(ki < qi)
    def _():
        run_staggered([block(h, rb, kb, False) for rb in range(nrb) for kb in range(nrb) for h in range(HEADS)])

    @pl.when(ki == qi)
    def _():
        run_staggered([block(h, rb, kb, kb == rb) for rb in range(nrb) for kb in range(rb + 1)
                       for h in range(HEADS)])
        for rb in range(nrb):
            o_t = jnp.concatenate([acc_scr[h * nrb + rb, 0:HD] / acc_scr[h * nrb + rb, HD:HD + 1]
                                   for h in range(HEADS)], axis=0)
            o_ref[0, rb * qb:(rb + 1) * qb, :] = o_t.T


def _flash_t(q, k, vt):
    b, t, wide = q.shape
    tile = min(1024, t)
    qb = min(256, tile)
    nrb = tile // qb
    pairs = [(i, j) for i in range(t // tile) for j in range(i + 1)]
    qt = jnp.asarray(np.array([p[0] for p in pairs], np.int32))
    kt = jnp.asarray(np.array([p[1] for p in pairs], np.int32))
    grid_spec = pltpu.PrefetchScalarGridSpec(
        num_scalar_prefetch=2,
        grid=(b, len(pairs)),
        in_specs=[pl.BlockSpec((1, tile, wide), lambda i, p, qt, kt: (i, qt[p], 0)),
                  pl.BlockSpec((1, tile, wide), lambda i, p, qt, kt: (i, kt[p], 0)),
                  pl.BlockSpec((1, HEADS * VT_ROWS, tile), lambda i, p, qt, kt: (i, 0, kt[p]))],
        out_specs=pl.BlockSpec((1, tile, BRANCH), lambda i, p, qt, kt: (i, qt[p], 0)),
        scratch_shapes=[pltpu.VMEM((HEADS * nrb, 1, qb), F32), pltpu.VMEM((HEADS * nrb, VT_ROWS, qb), F32)],
    )
    return pl.pallas_call(
        functools.partial(_flash_t_kernel, tile=tile, qb=qb),
        grid_spec=grid_spec,
        out_shape=jax.ShapeDtypeStruct((b, t, BRANCH), F32),
        compiler_params=_params(("parallel", "arbitrary")),
        name="flash_t",
    )(qt, kt, q, k, vt)


def _attend_cache_kernel(q_ref, k_ref, v_ref, o_ref, *, kv_valid):
    tq, tk = q_ref.shape[1], k_ref.shape[1]
    valid = lax.broadcasted_iota(jnp.int32, (tq, tk), 1) < kv_valid
    outs = []
    for h in range(HEADS):
        sl = slice(h * HEAD_PAD, (h + 1) * HEAD_PAD)
        s = lax.dot_general(q_ref[0, :, sl], k_ref[0, :, sl], (((1,), (1,)), ((), ())), preferred_element_type=F32)
        s = jnp.where(valid, s, -1e30)
        p = jnp.exp2(s - jnp.max(s, axis=-1, keepdims=True))
        pv = jnp.dot(p.astype(BF16), v_ref[0, :, sl], preferred_element_type=F32)
        outs.append(pv / jnp.sum(p, axis=-1, keepdims=True))
    o_ref[0, :, 0:LANE] = outs[0] + pltpu.roll(outs[1], HD, 1)
    o_ref[0, :, LANE:2 * LANE] = outs[2] + pltpu.roll(outs[3], HD, 1)


def _attend_cache(q, k, v, *, kv_valid):
    b, t, wide = q.shape
    n = k.shape[1]
    assert (t - 1) // CHUNK == 0 and (kv_valid - t) % CHUNK == 0, "queries must sit in the last key chunk"
    return pl.pallas_call(
        functools.partial(_attend_cache_kernel, kv_valid=kv_valid),
        grid=(b,),
        in_specs=[pl.BlockSpec((1, t, wide), lambda i: (i, 0, 0)), pl.BlockSpec((1, n, wide), lambda i: (i, 0, 0)),
                  pl.BlockSpec((1, n, wide), lambda i: (i, 0, 0))],
        out_specs=pl.BlockSpec((1, t, BRANCH), lambda i: (i, 0, 0)),
        out_shape=jax.ShapeDtypeStruct((b, t, BRANCH), F32),
        compiler_params=_params(("parallel",)),
        name="attend_cache",
    )(q, k, v)


def _combine_kernel(x_ref, ya_ref, yb_ref, zc_ref, zd_ref, halo_ref, zg_ref, lng_ref, lnb_ref, ws_ref, sb_ref,
                    pw_ref, psc_ref, wout_ref, *out_refs, tm, cs, n_hist, halo_is_prev_tile):
    out_ref = out_refs[0]
    ti = pl.program_id(1)
    lane = lax.broadcasted_iota(jnp.int32, (1, BRANCH), 1)

    zc = zc_ref[0]
    u = zc[:, 0:BRANCH]
    vraw = zc[:, BRANCH:]
    xc = vraw - jnp.mean(vraw, axis=-1, keepdims=True)
    var = jnp.mean(xc * xc, axis=-1, keepdims=True)
    vn = xc * lax.rsqrt(var + SGU_EPS) * lng_ref[...] + lnb_ref[...]
    if len(out_refs) > 1:
        out_refs[1][0] = vn
    vnb = vn.astype(BF16)
    zero = jnp.zeros((), BF16)
    parts = []
    for n in range(tm // cs):
        vch = vnb[n * cs:(n + 1) * cs]
        stacked = jnp.concatenate([jnp.where(lane // HD == h, vch, zero) for h in range(HEADS)], axis=0)
        parts.append(jnp.dot(ws_ref[...], stacked, preferred_element_type=F32) + sb_ref[...])
    yc = u * (parts[0] if len(parts) == 1 else jnp.concatenate(parts, axis=0))

    zd = zd_ref[0]
    halo = halo_ref[0]
    if halo_is_prev_tile:
        halo = jnp.where(ti == 0, 0.0, halo)
    ext = jnp.concatenate([halo, zd], axis=0)
    sums = []
    acc = ext
    for sh in (1, 2, 4, 8):
        acc = acc + pltpu.roll(acc, sh, 0)
        sums.append(acc[HALO:])
    t_idx = lax.broadcasted_iota(jnp.int32, (tm, BRANCH), 0) + (ti * tm + n_hist + 1)
    lane2 = lax.broadcasted_iota(jnp.int32, (tm, BRANCH), 1)
    win = jnp.where(lane2 < HD, POOL_WINDOWS[0],
                    jnp.where(lane2 < 2 * HD, POOL_WINDOWS[1],
                              jnp.where(lane2 < 3 * HD, POOL_WINDOWS[2], POOL_WINDOWS[3])))
    cnt = jnp.minimum(t_idx, win).astype(F32)
    ssum = jnp.where(lane2 < HD, sums[0],
                     jnp.where(lane2 < 2 * HD, sums[1], jnp.where(lane2 < 3 * HD, sums[2], sums[3])))
    d = ssum / cnt - zd
    yd = jnp.dot(d.astype(BF16), pw_ref[...], preferred_element_type=F32) * psc_ref[...]

    g = zg_ref[0]
    gate = g / (1.0 + jnp.exp(-g))
    y = jnp.concatenate([ya_ref[0], yb_ref[0], yc, yd], axis=-1) * gate
    out_ref[0] = x_ref[0] + jnp.dot(y.astype(BF16), wout_ref[...], preferred_element_type=F32)


def _combine(x, ya, yb, zc, zd, halo, zg, lp, *, n_hist, want_vn):
    b, t, _ = x.shape
    tm = min(512, t)
    cs = min(SGU_CHUNK, t)
    halo_is_prev_tile = halo is None
    if halo_is_prev_tile:
        halo = zd
        per = tm // HALO
        halo_spec = pl.BlockSpec((1, HALO, BRANCH), lambda i, j: (i, jnp.maximum(j * per - 1, 0), 0))
    else:
        halo_spec = pl.BlockSpec((1, HALO, BRANCH), lambda i, j: (i, 0, 0))
    blk = lambda w: pl.BlockSpec((1, tm, w), lambda i, j: (i, j, 0))
    row = lambda w: _full((1, w))
    out_specs = [blk(D_MODEL)]
    out_shape = [jax.ShapeDtypeStruct((b, t, D_MODEL), F32)]
    if want_vn:
        out_specs.append(blk(BRANCH))
        out_shape.append(jax.ShapeDtypeStruct((b, t, BRANCH), F32))
    res = pl.pallas_call(
        functools.partial(_combine_kernel, tm=tm, cs=cs, n_hist=n_hist, halo_is_prev_tile=halo_is_prev_tile),
        grid=(b, t // tm),
        in_specs=[blk(D_MODEL), blk(BRANCH), blk(BRANCH), blk(2 * BRANCH), blk(BRANCH), halo_spec, blk(D_MODEL),
                  row(BRANCH), row(BRANCH), _full((cs, HEADS * cs)), _full((cs, BRANCH)),
                  _full((BRANCH, BRANCH)), row(BRANCH), _full((D_MODEL, D_MODEL))],
        out_specs=out_specs,
        out_shape=out_shape,
        compiler_params=_params(("parallel", "parallel")),
        name="combine",
    )(x, ya, yb, zc, zd, halo, zg, lp["sgu_ln_g"], lp["sgu_ln_b"], lp["sgu_w_cat"][cs], lp["sgu_b_tab"][cs],
      lp["pool_w_bd"], lp["pool_scale"], lp["w_out"])
    return res


def _pad_cols(w, total):
    return jnp.pad(w, ((0, 0), (0, total - w.shape[1])))


def _head_slots(w, real):
    rows = w.shape[0]
    w = w.reshape(rows, HEADS, real)
    return jnp.pad(w, ((0, 0), (0, 0), (0, HEAD_PAD - real))).reshape(rows, HEADS * HEAD_PAD)


def _packed_params(*stacked, sgu_sizes):
    packed = jax.vmap(functools.partial(_pack_layer, sgu_sizes=sgu_sizes))(*stacked)
    return [jax.tree.map(lambda v: v[l], packed) for l in range(stacked[0].shape[0])]


def _pack_layer(norm_g, w_in, w_out, rw_mu, rw_w0, rw_w2, rw_a0, rw_a2, rw_kk, rw_ka, rw_rk, rw_gn_g,
                rw_gn_b, mla_qa_g, mla_w_uq, mla_kva_g, mla_w_uk, mla_w_uv, mla_q_norm_g, mla_k_norm_g, sgu_w,
                sgu_b, sgu_ln_g, sgu_ln_b, pool_w, pool_scale, *, sgu_sizes):
    wi = w_in
    off_b = RW_SHIFT
    off_c = off_b + Q_RANK + KV_RANK + ROPE
    off_d = off_c + 2 * BRANCH
    off_g = off_d + BRANCH
    w_in_r = jnp.concatenate([
        wi[:, :off_b],
        _pad_cols(wi[:, off_b:off_b + Q_RANK], Q_PAD),
        wi[:, off_b + Q_RANK:off_b + Q_RANK + KV_RANK],
        _pad_cols(wi[:, off_b + Q_RANK + KV_RANK:off_c], LANE),
        wi[:, off_c:off_d], wi[:, off_d:off_g], wi[:, off_g:]], axis=1).astype(BF16)
    zeros = jnp.zeros((RW_LORA, BRANCH), F32)
    rw_lora = jnp.concatenate([jnp.concatenate([rw_w2, zeros], 1), jnp.concatenate([zeros, rw_a2], 1)], 0)
    row = lambda v: v.reshape(1, -1).astype(F32)
    qn = mla_q_norm_g
    kn = mla_k_norm_g
    tri = jnp.tril(jnp.ones((SGU_CHUNK, SGU_CHUNK), bool))
    ws = jnp.where(tri[None], sgu_w, 0.0)
    sgu_w_cat = {cs: jnp.concatenate([ws[h, :cs, :cs] for h in range(HEADS)], axis=1).astype(BF16)
                 for cs in sgu_sizes}
    sgu_b_tab = {cs: jnp.repeat(jnp.transpose(sgu_b)[:cs], HD, axis=1) for cs in sgu_sizes}
    eye = jnp.eye(HEADS, dtype=F32)
    pool_w_bd = jnp.einsum("gcd,gh->gchd", pool_w, eye).reshape(BRANCH, BRANCH).astype(BF16)
    return {
        "norm_g": row(norm_g), "w_in": w_in_r, "w_out": w_out.astype(BF16),
        "rw_mu": row(rw_mu), "rw_lora": rw_lora.astype(BF16), "rw_w0": row(rw_w0), "rw_a0": row(rw_a0),
        "rw_kk": row(rw_kk), "rw_ka": row(rw_ka), "rw_rk": row(rw_rk), "rw_gn_g": row(rw_gn_g),
        "rw_gn_b": row(rw_gn_b),
        "mla_qa_g": _pad_cols(row(mla_qa_g), Q_PAD),
        "mla_w_uq": jnp.pad(_head_slots(mla_w_uq, QK_HD), ((0, Q_PAD - Q_RANK), (0, 0))).astype(BF16),
        "mla_qn_g": _pad_cols(row(qn), HEAD_PAD),
        "mla_kva_g": row(mla_kva_g),
        "mla_kr_g": _pad_cols(row(kn[NOPE:]), LANE),
        "mla_w_uk": _head_slots(mla_w_uk, NOPE).astype(BF16),
        "mla_kn_g": _pad_cols(row(kn[:NOPE]), HEAD_PAD),
        "mla_w_uv": _head_slots(mla_w_uv, HD).astype(BF16),
        "mla_w_uv_t": jnp.transpose(mla_w_uv).astype(BF16),
        "sgu_ln_g": row(sgu_ln_g), "sgu_ln_b": row(sgu_ln_b), "sgu_w_cat": sgu_w_cat, "sgu_b_tab": sgu_b_tab,
        "pool_w_bd": pool_w_bd, "pool_scale": row(pool_scale),
    }


def _rope_tables(pos):
    half = ROPE // 2
    inv = ROPE_THETA ** (-jnp.arange(half, dtype=F32) / half)
    ang = pos.astype(F32)[:, None] * inv[None, :]
    cos, sin = jnp.cos(ang), jnp.sin(ang)
    t = pos.shape[0]

    def place(first, second, off, fill):
        base = jnp.full((t, LANE), fill, F32)
        base = lax.dynamic_update_slice(base, first, (0, off))
        return lax.dynamic_update_slice(base, second, (0, off + half))

    z = jnp.zeros_like(sin)
    tabs = []
    for off, fill in ((NOPE, 1.0), (0, 0.0)):
        tabs += [place(cos, cos, off, fill), place(-sin, z, off, 0.0), place(z, sin, off, 0.0)]
    return tabs


def _state_to_blockdiag(s):
    b = s.shape[0]
    eye = jnp.eye(HEADS, dtype=s.dtype)
    return jnp.einsum("bhij,hg->bhjgi", s, eye).reshape(b, BRANCH, BRANCH)


def _blockdiag_to_state(z):
    b = z.shape[0]
    z5 = z.reshape(b, HEADS, HD, HEADS, HD)
    return jnp.stack([jnp.swapaxes(z5[:, h, :, h, :], 1, 2) for h in range(HEADS)], axis=1)


def _layer(x, lp, tabs, hist):
    b, t, _ = x.shape
    za, zb, zc, zd, zg = _inproj(x.reshape(b * t, D_MODEL), lp["norm_g"], lp["w_in"])
    za = za.reshape(b, t, RW_SHIFT)
    zb = zb.reshape(b, t, 512)
    zc = zc.reshape(b, t, 2 * BRANCH)
    zd = zd.reshape(b, t, BRANCH)
    zg = zg.reshape(b, t, D_MODEL)
    if hist is None:
        shift_prev = jnp.zeros((b, 1, RW_SHIFT), F32)
        zbd0 = jnp.zeros((b, BRANCH, BRANCH), F32)
    else:
        c_ckv, c_kr, s_wkv, s_shift, s_pool = hist
        shift_prev = s_shift[:, None, :]
        zbd0 = _state_to_blockdiag(s_wkv)
    ya, zfin = _rwkv(za, shift_prev, zbd0, lp)
    q, ckv, krp, kr32 = _mla_q(zb, tabs, lp)
    if hist is None:
        k, vt = _mla_kv(ckv, krp, lp, transposed=True)
        yb = _flash_t(q, k, vt)
        halo, n_hist = None, 0
        pool_src = zd
    else:
        past = c_ckv.shape[1]
        n_all = past + t
        n_pad = -(-n_all // LANE) * LANE
        ckv_all = jnp.pad(jnp.concatenate([c_ckv, ckv], axis=1), ((0, 0), (0, n_pad - n_all), (0, 0)))
        kr_all = jnp.concatenate([jnp.pad(c_kr, ((0, 0), (0, 0), (0, LANE - ROPE))), krp], axis=1)
        kr_all = jnp.pad(kr_all, ((0, 0), (0, n_pad - n_all), (0, 0)))
        k, v = _mla_kv(ckv_all, kr_all, lp, transposed=False)
        yb = _attend_cache(q, k, v, kv_valid=n_all)
        halo = jnp.pad(s_pool, ((0, 0), (HALO - POOL_HIST, 0), (0, 0)))
        n_hist = POOL_HIST
        pool_src = jnp.concatenate([s_pool, zd], axis=1)
    res = _combine(x, ya, yb, zc, zd, halo, zg, lp, n_hist=n_hist, want_vn=hist is not None)
    state = (ckv, kr32, _blockdiag_to_state(zfin), za[:, -1], pool_src[:, pool_src.shape[1] - POOL_HIST:])
    if hist is None:
        return res[0], state
    return res[0], state + (res[1],)


def kernel(x_prompt, x_sample, cache_ckv, cache_krope, state_wkv, state_shift, state_pool, norm_g, w_in, w_out, rw_mu, rw_w0, rw_w2, rw_a0, rw_a2, rw_kk, rw_ka, rw_rk, rw_gn_g, rw_gn_b, mla_qa_g, mla_w_uq, mla_kva_g, mla_w_uk, mla_w_uv, mla_q_norm_g, mla_k_norm_g, sgu_w, sgu_b, sgu_ln_g, sgu_ln_b, pool_w, pool_scale):
    t_p = x_prompt.shape[1]
    t_s = x_sample.shape[1]
    past = cache_ckv.shape[2]
    sgu_sizes = sorted({min(SGU_CHUNK, t_p), min(SGU_CHUNK, t_s)})
    tabs_p = _rope_tables(jnp.arange(t_p))
    tabs_s = _rope_tables(past + jnp.arange(t_s))
    yp, ys = x_prompt, x_sample
    new_p = [[] for _ in range(5)]
    new_s = [[] for _ in range(6)]
    params = _packed_params(norm_g, w_in, w_out, rw_mu, rw_w0, rw_w2, rw_a0, rw_a2, rw_kk, rw_ka, rw_rk, rw_gn_g,
                            rw_gn_b, mla_qa_g, mla_w_uq, mla_kva_g, mla_w_uk, mla_w_uv, mla_q_norm_g, mla_k_norm_g,
                            sgu_w, sgu_b, sgu_ln_g, sgu_ln_b, pool_w, pool_scale, sgu_sizes=sgu_sizes)
    for l, lp in enumerate(params):
        yp, st_p = _layer(yp, lp, tabs_p, None)
        ys, st_s = _layer(ys, lp, tabs_s, (cache_ckv[l], cache_krope[l], state_wkv[l], state_shift[l], state_pool[l]))
        for lst, arr in zip(new_p, st_p):
            lst.append(arr)
        for lst, arr in zip(new_s, st_s):
            lst.append(arr)
    outs_p = [jnp.stack(a, axis=0) for a in new_p]
    outs_s = [jnp.stack(a, axis=0) for a in new_s]
    return (yp, ys, *outs_p, *outs_s)
```

```python
import functools
import math

import jax
import jax.numpy as jnp
import numpy as np
from jax import lax
from jax.experimental import pallas as pl
from jax.experimental.pallas import tpu as pltpu

F32 = jnp.float32
BF16 = jnp.bfloat16

D_MODEL = 1024
CHUNK = 64
BRANCH = 256
NORM_EPS = 1e-6
HEADS = 4
HD = 64
RW_LORA = 64
RW_SHIFT = 3 * BRANCH + 2 * RW_LORA
RW_GN_EPS = 64e-5
NOPE = 64
ROPE = 32
QK_HD = NOPE + ROPE
Q_RANK = 192
KV_RANK = 128
ROPE_THETA = 10000.0
SGU_CHUNK = 128
SGU_EPS = 1e-5
POOL_WINDOWS = (2, 4, 8, 16)
POOL_HIST = 15
D_IN = 3040

LANE = 128
HALO = 16
Q_PAD = 256
HEAD_PAD = 128
VMEM_LIMIT = 48 * 1024 * 1024

SEG_A = (0, 896)
SEG_B = (896, 1408)
SEG_C = (1408, 1920)
SEG_D = (1920, 2176)
SEG_G = (2176, 3200)
D_IN_PAD = 3200


def _params(sem):
    return pltpu.CompilerParams(dimension_semantics=sem, vmem_limit_bytes=VMEM_LIMIT)


def _full(shape):
    nd = len(shape)
    return pl.BlockSpec(shape, lambda *_: (0,) * nd)


def _inproj_kernel(x_ref, g_ref, w_ref, za_ref, zb_ref, zc_ref, zd_ref, zg_ref):
    x = x_ref[...]
    ms = jnp.mean(x * x, axis=-1, keepdims=True)
    xn = (x * lax.rsqrt(ms + NORM_EPS) * g_ref[...]).astype(BF16)
    for ref, (lo, hi) in ((za_ref, SEG_A), (zb_ref, SEG_B), (zc_ref, SEG_C), (zd_ref, SEG_D), (zg_ref, SEG_G)):
        ref[...] = jnp.dot(xn, w_ref[:, lo:hi], preferred_element_type=F32)


def _inproj(x2d, g, w):
    n = x2d.shape[0]
    tm = min(512, n)
    widths = [hi - lo for lo, hi in (SEG_A, SEG_B, SEG_C, SEG_D, SEG_G)]
    return pl.pallas_call(
        _inproj_kernel,
        grid=(n // tm,),
        in_specs=[pl.BlockSpec((tm, D_MODEL), lambda i: (i, 0)), _full((1, D_MODEL)), _full((D_MODEL, D_IN_PAD))],
        out_specs=[pl.BlockSpec((tm, wd), lambda i: (i, 0)) for wd in widths],
        out_shape=[jax.ShapeDtypeStruct((n, wd), F32) for wd in widths],
        compiler_params=_params(("parallel",)),
        name="inproj",
    )(x2d, g, w)


def _softplus(x):
    return jnp.maximum(x, 0.0) + jnp.log(1.0 + jnp.exp(-jnp.abs(x)))


def _split_bf16(x, parts):
    out = []
    for _ in range(parts - 1):
        hi = x.astype(BF16)
        out.append(hi)
        x = x - hi.astype(F32)
    out.append(x.astype(BF16))
    return out


def _head_sum(x, ones_bd):
    return sum(jnp.dot(p, ones_bd, preferred_element_type=F32) for p in _split_bf16(x, 2))


def _dot_s(a, b):
    return jnp.dot(a.astype(BF16), b.astype(BF16), preferred_element_type=F32)


def _dot_nt_s(a, b):
    return lax.dot_general(a.astype(BF16), b.astype(BF16), (((1,), (1,)), ((), ())), preferred_element_type=F32)


def _rwkv_kernel(za_ref, shift_ref, z0_ref, mu_ref, wl_ref, w0_ref, a0_ref, kkg_ref, kag_ref, rkg_ref,
                 gng_ref, gnb_ref, ya_ref, zfin_ref, z_scr, carry_scr, *, c, levels, rows):
    ci = pl.program_id(1)

    @pl.when(ci == 0)
    def _():
        z_scr[...] = z0_ref[...]
        for i in range(rows):
            carry_scr[i] = jnp.broadcast_to(shift_ref[i], carry_scr.shape[1:])

    subs = za_ref.shape[1] // c
    states = {}
    consts = _rwkv_consts(c)
    gens, starts = [], []
    for s in range(subs):
        for i in range(rows):
            za = za_ref[i, s * c:(s + 1) * c, :]
            carry = carry_scr[i, 0:1, :] if s == 0 else za_ref[i, s * c - 1:s * c, :]
            get_z = (lambda i=i: z_scr[i]) if s == 0 else (lambda i=i, s=s: states[(i, s - 1)])
            publish = lambda z_new, i=i, s=s: states.__setitem__((i, s), z_new)
            gens.append(_rwkv_chunk(za, carry, get_z, publish, consts, mu_ref, wl_ref, w0_ref, a0_ref, kkg_ref, kag_ref,
                                    rkg_ref, gng_ref, gnb_ref, c=c, levels=levels))
            starts.append(s * RWKV_SKEW)
    yas = _run_lockstep(gens, starts)
    ya_ref[...] = jnp.stack([jnp.concatenate([yas[s * rows + i] for s in range(subs)], axis=0)
                             for i in range(rows)], axis=0)
    z_scr[...] = jnp.stack([states[(i, subs - 1)] for i in range(rows)], axis=0)
    carry_scr[...] = jnp.stack([jnp.broadcast_to(za_ref[i, subs * c - 1:subs * c, :], carry_scr.shape[1:])
                                for i in range(rows)], axis=0)

    @pl.when(ci == pl.num_programs(1) - 1)
    def _():
        zfin_ref[...] = z_scr[...]


def _rwkv_consts(c):
    hc = HEADS * c
    lane = lax.broadcasted_iota(jnp.int32, (c, BRANCH), 1)
    rr = lax.broadcasted_iota(jnp.int32, (BRANCH, BRANCH), 0)
    cc = lax.broadcasted_iota(jnp.int32, (BRANCH, BRANCH), 1)
    tr = lax.broadcasted_iota(jnp.int32, (c, c), 0)
    tc = lax.broadcasted_iota(jnp.int32, (c, c), 1)
    wt = lax.broadcasted_iota(jnp.int32, (c, hc), 0)
    wl = lax.broadcasted_iota(jnp.int32, (c, hc), 1)
    ws = wl % c
    bits = lambda cond: jnp.where(cond, 1.0, 0.0).astype(BF16)
    return {
        "head_block": rr // HD == cc // HD,
        "ones_bd": bits(rr // HD == cc // HD),
        "tril": bits(tc <= tr),
        "head_lanes": [bits(lane // HD == h) for h in range(HEADS)],
        "head_cols": [bits(wl // c == h) for h in range(HEADS)],
        "strict": ws < wt, "incl": ws <= wt,
        "eye_w": jnp.where(ws == wt, 1.0, 0.0).astype(F32),
    }


def _rwkv_chunk(za, carry, get_z, publish, consts, mu_ref, wl_ref, w0_ref, a0_ref, kkg_ref, kag_ref, rkg_ref,
                gng_ref, gnb_ref, *, c, levels):
    row = lax.broadcasted_iota(jnp.int32, za.shape, 0)
    prev = jnp.where(row == 0, carry, pltpu.roll(za, 1, 0))
    zs = za + mu_ref[...] * (prev - za)
    r = zs[:, 0:BRANCH]
    k = zs[:, BRANCH:2 * BRANCH]
    v = zs[:, 2 * BRANCH:3 * BRANCH]
    wa = zs[:, 3 * BRANCH:RW_SHIFT]
    lane128 = lax.broadcasted_iota(jnp.int32, wa.shape, 1)
    lora_in = jnp.where(lane128 < RW_LORA, jnp.tanh(wa), wa).astype(BF16)
    lora = jnp.dot(lora_in, wl_ref[...], preferred_element_type=F32)
    w_log = -_softplus(-(w0_ref[...] + lora[:, :BRANCH])) - 0.5
    logd = -jnp.exp(w_log)
    asig = 1.0 / (1.0 + jnp.exp(-(a0_ref[...] + lora[:, BRANCH:])))

    ones_bd = consts["ones_bd"]

    kkv = k * kkg_ref[...]
    kk = kkv * lax.rsqrt(jnp.maximum(_head_sum(kkv * kkv, ones_bd), 1e-24))
    kmod = k * (1.0 + (asig - 1.0) * kag_ref[...])
    avec = -kk
    bvec = kk * asig

    lcum = sum(jnp.dot(consts["tril"], p, preferred_element_type=F32) for p in _split_bf16(logd, 3))
    lexc = lcum - logd
    ltot = lcum[c - 1:c, :]
    g_in = jnp.exp(lcum)
    g_ex = jnp.exp(lexc)
    g_inv = jnp.exp(-lcum)
    g_rem = jnp.exp(ltot - lcum)

    def stack(x):
        xb = x.astype(BF16)
        return jnp.concatenate([xb * m for m in consts["head_lanes"]], axis=0)

    strict = lambda x: jnp.where(consts["strict"], x, 0.0)
    incl = lambda x: jnp.where(consts["incl"], x, 0.0)

    def stack_w(x):
        xb = x.astype(BF16)
        return jnp.concatenate([xb * m for m in consts["head_cols"]], axis=0)

    a_t = avec * g_ex
    r_t = r * g_in
    b_s = stack(bvec * g_inv)
    k_s = stack(kmod * g_inv)
    v_s = stack(v)
    yield
    a_ab = strict(_dot_nt_s(a_t, b_s))
    a_ak = strict(_dot_nt_s(a_t, k_s))
    a_rb = incl(_dot_nt_s(r_t, b_s))
    a_rk = incl(_dot_nt_s(r_t, k_s))
    yield

    t_w = consts["eye_w"] + a_ab
    pw = a_ab
    bd = stack_w(pw)
    for _ in range(levels - 1):
        pw = _dot_s(pw, bd)
        yield
        bd = stack_w(pw)
        t_w = t_w + _dot_s(t_w, bd)

    g_col = jnp.exp(jnp.sum(logd.T, axis=1, keepdims=True))
    pad = [jnp.zeros((max(LANE - 2 * c, 0) // 2, BRANCH), F32)] * 2 if 2 * c < LANE else []
    lhs_t = jnp.concatenate([bvec * g_rem, kmod * g_rem] + pad, axis=0).T
    yield
    z = get_z()
    w_z = _dot_s(a_t, z) + _dot_s(a_ak, v_s)
    y_part = _dot_s(r_t, z) + _dot_s(a_rk, v_s)
    yield
    u = _dot_s(t_w, stack(w_z))
    yield
    y = y_part + _dot_s(a_rb, stack(u))
    upd = _dot_s(lhs_t, jnp.concatenate([u, v] + pad, axis=0))
    z_new = z * g_col + jnp.where(consts["head_block"], upd, 0.0)
    publish(z_new)
    yield

    mean = _head_sum(y, ones_bd) * (1.0 / HD)
    yield
    yc = y - mean
    var = _head_sum(yc * yc, ones_bd) * (1.0 / HD)
    yn = yc * lax.rsqrt(var + RW_GN_EPS) * gng_ref[...] + gnb_ref[...]
    bonus = _head_sum(r * kmod * rkg_ref[...], ones_bd) * v
    return yn + bonus


def _run_lockstep(gens, starts):
    results = [None] * len(gens)
    done = [False] * len(gens)
    tick = 0
    while not all(done):
        for i, g in enumerate(gens):
            if done[i] or tick < starts[i]:
                continue
            try:
                next(g)
            except StopIteration as stop:
                results[i], done[i] = stop.value, True
        tick += 1
    return results


RWKV_SKEW = 3
RWKV_ROWS = 4
RWKV_SUBS = 4


def _rwkv(za, shift_prev, zbd0, lp):
    b, t, _ = za.shape
    c = min(CHUNK, t)
    levels = int(math.log2(c))
    rows = math.gcd(RWKV_ROWS, b)
    tb = c * math.gcd(RWKV_SUBS, t // c)
    row = lambda w: _full((1, w))
    return pl.pallas_call(
        functools.partial(_rwkv_kernel, c=c, levels=levels, rows=rows),
        grid=(b // rows, t // tb),
        in_specs=[pl.BlockSpec((rows, tb, RW_SHIFT), lambda i, j: (i, j, 0)),
                  pl.BlockSpec((rows, 1, RW_SHIFT), lambda i, j: (i, 0, 0)),
                  pl.BlockSpec((rows, BRANCH, BRANCH), lambda i, j: (i, 0, 0)),
                  row(RW_SHIFT), _full((2 * RW_LORA, 2 * BRANCH)), row(BRANCH), row(BRANCH), row(BRANCH),
                  row(BRANCH), row(BRANCH), row(BRANCH), row(BRANCH)],
        out_specs=[pl.BlockSpec((rows, tb, BRANCH), lambda i, j: (i, j, 0)),
                   pl.BlockSpec((rows, BRANCH, BRANCH), lambda i, j: (i, 0, 0))],
        out_shape=[jax.ShapeDtypeStruct((b, t, BRANCH), F32), jax.ShapeDtypeStruct((b, BRANCH, BRANCH), F32)],
        scratch_shapes=[pltpu.VMEM((rows, BRANCH, BRANCH), F32), pltpu.VMEM((rows, 8, RW_SHIFT), F32)],
        compiler_params=_params(("parallel", "arbitrary")),
        name="rwkv",
    )(za, shift_prev, zbd0, lp["rw_mu"], lp["rw_lora"], lp["rw_w0"], lp["rw_a0"], lp["rw_kk"], lp["rw_ka"],
      lp["rw_rk"], lp["rw_gn_g"], lp["rw_gn_b"])


def _rope_apply(x, cos, s1, s2):
    half = ROPE // 2
    return x * cos + pltpu.roll(x, LANE - half, 1) * s1 + pltpu.roll(x, half, 1) * s2


def _mla_q_kernel(zb_ref, cq_ref, s1q_ref, s2q_ref, ck_ref, s1k_ref, s2k_ref, gqa_ref, wuq_ref, gqn_ref,
                  gkva_ref, gkr_ref, q_ref, ckv_ref, kr_ref, kr32_ref):
    z = zb_ref[0]
    zq = z[:, 0:Q_PAD]
    qa = zq * lax.rsqrt(jnp.sum(zq * zq, axis=-1, keepdims=True) * (1.0 / Q_RANK) + NORM_EPS) * gqa_ref[...]
    q = jnp.dot(qa.astype(BF16), wuq_ref[...], preferred_element_type=F32)
    lane = lax.broadcasted_iota(jnp.int32, (1, HEAD_PAD), 1)
    is_nope = lane < NOPE
    sm_scale = math.log2(math.e) / math.sqrt(QK_HD)
    for h in range(HEADS):
        qh = q[:, h * HEAD_PAD:(h + 1) * HEAD_PAD]
        sq = qh * qh
        sn = jnp.sum(jnp.where(is_nope, sq, 0.0), axis=-1, keepdims=True) * (1.0 / NOPE)
        sq_rope = jnp.where(is_nope, 0.0, jnp.where(lane < QK_HD, sq, 0.0))
        sr = jnp.sum(sq_rope, axis=-1, keepdims=True) * (1.0 / ROPE)
        scale = jnp.where(is_nope, lax.rsqrt(sn + NORM_EPS), lax.rsqrt(sr + NORM_EPS))
        xr = qh * scale * gqn_ref[...]
        out = _rope_apply(xr, cq_ref[...], s1q_ref[...], s2q_ref[...])
        q_ref[0, :, h * HEAD_PAD:(h + 1) * HEAD_PAD] = (out * sm_scale).astype(BF16)
    zkv = z[:, Q_PAD:Q_PAD + KV_RANK]
    ckv_ref[0] = zkv * lax.rsqrt(jnp.mean(zkv * zkv, axis=-1, keepdims=True) + NORM_EPS) * gkva_ref[...]
    zkr = z[:, Q_PAD + KV_RANK:]
    xr = zkr * lax.rsqrt(jnp.sum(zkr * zkr, axis=-1, keepdims=True) * (1.0 / ROPE) + NORM_EPS) * gkr_ref[...]
    kr = _rope_apply(xr, ck_ref[...], s1k_ref[...], s2k_ref[...])
    kr_ref[0] = kr
    kr32_ref[0] = kr[:, 0:ROPE]


def _mla_q(zb, tabs, lp):
    b, t, _ = zb.shape
    tm = min(1024, t)
    tab = pl.BlockSpec((tm, LANE), lambda i, j: (j, 0))
    row = lambda w: _full((1, w))
    return pl.pallas_call(
        _mla_q_kernel,
        grid=(b, t // tm),
        in_specs=[pl.BlockSpec((1, tm, 512), lambda i, j: (i, j, 0)), tab, tab, tab, tab, tab, tab,
                  row(Q_PAD), _full((Q_PAD, HEADS * HEAD_PAD)), row(HEAD_PAD), row(KV_RANK), row(LANE)],
        out_specs=[pl.BlockSpec((1, tm, HEADS * HEAD_PAD), lambda i, j: (i, j, 0)),
                   pl.BlockSpec((1, tm, KV_RANK), lambda i, j: (i, j, 0)),
                   pl.BlockSpec((1, tm, LANE), lambda i, j: (i, j, 0)),
                   pl.BlockSpec((1, tm, ROPE), lambda i, j: (i, j, 0))],
        out_shape=[jax.ShapeDtypeStruct((b, t, HEADS * HEAD_PAD), BF16),
                   jax.ShapeDtypeStruct((b, t, KV_RANK), F32),
                   jax.ShapeDtypeStruct((b, t, LANE), F32),
                   jax.ShapeDtypeStruct((b, t, ROPE), F32)],
        compiler_params=_params(("parallel", "parallel")),
        name="mla_q",
    )(zb, *tabs, lp["mla_qa_g"], lp["mla_w_uq"], lp["mla_qn_g"], lp["mla_kva_g"], lp["mla_kr_g"])


def _mla_kv_kernel(ckv_ref, kr_ref, wuk_ref, gkn_ref, wuv_ref, k_ref, v_ref, *, transposed):
    cb = ckv_ref[0].astype(BF16)
    kn = jnp.dot(cb, wuk_ref[...], preferred_element_type=F32)
    krs = pltpu.roll(kr_ref[0], NOPE, 1)
    for h in range(HEADS):
        kh = kn[:, h * HEAD_PAD:(h + 1) * HEAD_PAD]
        ms = jnp.sum(kh * kh, axis=-1, keepdims=True) * (1.0 / NOPE)
        k_ref[0, :, h * HEAD_PAD:(h + 1) * HEAD_PAD] = (kh * lax.rsqrt(ms + NORM_EPS) * gkn_ref[...] + krs).astype(BF16)
    if transposed:
        v = lax.dot_general(wuv_ref[...], cb, (((1,), (1,)), ((), ())), preferred_element_type=F32)
        ones = jnp.ones((VT_ONES, v.shape[1]), F32)
        v = jnp.concatenate([x for h in range(HEADS) for x in (v[h * HD:(h + 1) * HD], ones)], axis=0)
    else:
        v = jnp.dot(cb, wuv_ref[...], preferred_element_type=F32)
    v_ref[0] = v.astype(BF16)


def _mla_kv(ckv, kr, lp, *, transposed):
    b, n, _ = ckv.shape
    tm = 2048 if n % 2048 == 0 else n
    wide = HEADS * HEAD_PAD
    if transposed:
        wuv, v_spec = lp["mla_w_uv_t"], pl.BlockSpec((1, HEADS * VT_ROWS, tm), lambda i, j: (i, 0, j))
        v_shape = jax.ShapeDtypeStruct((b, HEADS * VT_ROWS, n), BF16)
    else:
        wuv, v_spec = lp["mla_w_uv"], pl.BlockSpec((1, tm, wide), lambda i, j: (i, j, 0))
        v_shape = jax.ShapeDtypeStruct((b, n, wide), BF16)
    return pl.pallas_call(
        functools.partial(_mla_kv_kernel, transposed=transposed),
        grid=(b, n // tm),
        in_specs=[pl.BlockSpec((1, tm, KV_RANK), lambda i, j: (i, j, 0)),
                  pl.BlockSpec((1, tm, LANE), lambda i, j: (i, j, 0)),
                  _full((KV_RANK, wide)), _full((1, HEAD_PAD)), _full(wuv.shape)],
        out_specs=[pl.BlockSpec((1, tm, wide), lambda i, j: (i, j, 0)), v_spec],
        out_shape=[jax.ShapeDtypeStruct((b, n, wide), BF16), v_shape],
        compiler_params=_params(("parallel", "parallel")),
        name="mla_kv",
    )(ckv, kr, lp["mla_w_uk"], lp["mla_kn_g"], wuv)


VT_ONES = 16
VT_ROWS = HD + VT_ONES
FLASH_AHEAD = 8
FLASH_BEHIND = 2


def _col_reduce(x, op, final):
    n = x.shape[0]
    step = n & -n
    parts = [x[i:i + step] for i in range(0, n, step)]
    while len(parts) > 1:
        parts = [op(parts[i], parts[i + 1]) if i + 1 < len(parts) else parts[i] for i in range(0, len(parts), 2)]
    x = parts[0]
    while x.shape[0] > 8:
        half = x.shape[0] // 2
        x = op(x[:half], x[half:])
    return final(x, axis=0, keepdims=True)


def _flash_t_kernel(qt_ref, kt_ref, q_ref, k_ref, vt_ref, o_ref, m_scr, acc_scr, *, tile, qb):
    p_id = pl.program_id(1)
    qi = qt_ref[p_id]
    ki = kt_ref[p_id]
    nrb = tile // qb
    nt = (((1,), (1,)), ((), ()))

    @pl.when(ki == 0)
    def _():
        m_scr[...] = jnp.full(m_scr.shape, -1e30, F32)
        acc_scr[...] = jnp.zeros(acc_scr.shape, F32)

    def block(h, rb, kb, masked):
        sl = slice(h * HEAD_PAD, (h + 1) * HEAD_PAD)
        keys = slice(kb * qb, (kb + 1) * qb)
        qh = q_ref[0, rb * qb:(rb + 1) * qb, sl]
        s = lax.dot_general(k_ref[0, keys, sl], qh, nt, preferred_element_type=F32)
        if masked:
            kr = lax.broadcasted_iota(jnp.int32, (qb, qb), 0)
            qc = lax.broadcasted_iota(jnp.int32, (qb, qb), 1)
            s = jnp.where((kr // CHUNK) <= (qc // CHUNK), s, -1e30)
        yield
        idx = h * nrb + rb
        m_prev = m_scr[idx]
        m_new = jnp.maximum(m_prev, _col_reduce(s, jnp.maximum, jnp.max))
        alpha = jnp.exp2(m_prev - m_new)
        m_scr[idx] = m_new
        pb = jnp.exp2(s - m_new).astype(BF16)
        yield
        pv = jnp.dot(vt_ref[0, h * VT_ROWS:(h + 1) * VT_ROWS, keys], pb, preferred_element_type=F32)
        acc_scr[idx] = alpha * acc_scr[idx] + pv

    def run_staggered(gens):
        n = len(gens)
        for j in range(n + FLASH_AHEAD + FLASH_BEHIND):
            for lag in (0, FLASH_AHEAD, FLASH_AHEAD + FLASH_BEHIND):
                if 0 <= j - lag < n:
                    next(gens[j - lag], None)

    @pl.when(ki < qi)
    def _():
        run_staggered([block(h, rb, kb, False) for rb in range(nrb) for kb in range(nrb) for h in range(HEADS)])

    @pl.when(ki == qi)
    def _():
        run_staggered([block(h, rb, kb, kb == rb) for rb in range(nrb) for kb in range(rb + 1)
                       for h in range(HEADS)])
        for rb in range(nrb):
            o_t = jnp.concatenate([acc_scr[h * nrb + rb, 0:HD] / acc_scr[h * nrb + rb, HD:HD + 1]
                                   for h in range(HEADS)], axis=0)
            o_ref[0, rb * qb:(rb + 1) * qb, :] = o_t.T


def _flash_t(q, k, vt):
    b, t, wide = q.shape
    tile = min(1024, t)
    qb = min(256, tile)
    nrb = tile // qb
    pairs = [(i, j) for i in range(t // tile) for j in range(i + 1)]
    qt = jnp.asarray(np.array([p[0] for p in pairs], np.int32))
    kt = jnp.asarray(np.array([p[1] for p in pairs], np.int32))
    grid_spec = pltpu.PrefetchScalarGridSpec(
        num_scalar_prefetch=2,
        grid=(b, len(pairs)),
        in_specs=[pl.BlockSpec((1, tile, wide), lambda i, p, qt, kt: (i, qt[p], 0)),
                  pl.BlockSpec((1, tile, wide), lambda i, p, qt, kt: (i, kt[p], 0)),
                  pl.BlockSpec((1, HEADS * VT_ROWS, tile), lambda i, p, qt, kt: (i, 0, kt[p]))],
        out_specs=pl.BlockSpec((1, tile, BRANCH), lambda i, p, qt, kt: (i, qt[p], 0)),
        scratch_shapes=[pltpu.VMEM((HEADS * nrb, 1, qb), F32), pltpu.VMEM((HEADS * nrb, VT_ROWS, qb), F32)],
    )
    return pl.pallas_call(
        functools.partial(_flash_t_kernel, tile=tile, qb=qb),
        grid_spec=grid_spec,
        out_shape=jax.ShapeDtypeStruct((b, t, BRANCH), F32),
        compiler_params=_params(("parallel", "arbitrary")),
        name="flash_t",
    )(qt, kt, q, k, vt)


def _attend_cache_kernel(q_ref, k_ref, v_ref, o_ref, *, kv_valid):
    tq, tk = q_ref.shape[1], k_ref.shape[1]
    valid = lax.broadcasted_iota(jnp.int32, (tq, tk), 1) < kv_valid
    outs = []
    for h in range(HEADS):
        sl = slice(h * HEAD_PAD, (h + 1) * HEAD_PAD)
        s = lax.dot_general(q_ref[0, :, sl], k_ref[0, :, sl], (((1,), (1,)), ((), ())), preferred_element_type=F32)
        s = jnp.where(valid, s, -1e30)
        p = jnp.exp2(s - jnp.max(s, axis=-1, keepdims=True))
        pv = jnp.dot(p.astype(BF16), v_ref[0, :, sl], preferred_element_type=F32)
        outs.append(pv / jnp.sum(p, axis=-1, keepdims=True))
    o_ref[0, :, 0:LANE] = outs[0] + pltpu.roll(outs[1], HD, 1)
    o_ref[0, :, LANE:2 * LANE] = outs[2] + pltpu.roll(outs[3], HD, 1)


def _attend_cache(q, k, v, *, kv_valid):
    b, t, wide = q.shape
    n = k.shape[1]
    assert (t - 1) // CHUNK == 0 and (kv_valid - t) % CHUNK == 0, "queries must sit in the last key chunk"
    return pl.pallas_call(
        functools.partial(_attend_cache_kernel, kv_valid=kv_valid),
        grid=(b,),
        in_specs=[pl.BlockSpec((1, t, wide), lambda i: (i, 0, 0)), pl.BlockSpec((1, n, wide), lambda i: (i, 0, 0)),
                  pl.BlockSpec((1, n, wide), lambda i: (i, 0, 0))],
        out_specs=pl.BlockSpec((1, t, BRANCH), lambda i: (i, 0, 0)),
        out_shape=jax.ShapeDtypeStruct((b, t, BRANCH), F32),
        compiler_params=_params(("parallel",)),
        name="attend_cache",
    )(q, k, v)


def _combine_kernel(x_ref, ya_ref, yb_ref, zc_ref, zd_ref, halo_ref, zg_ref, lng_ref, lnb_ref, ws_ref, sb_ref,
                    pw_ref, psc_ref, wout_ref, *out_refs, tm, cs, n_hist, halo_is_prev_tile):
    out_ref = out_refs[0]
    ti = pl.program_id(1)
    lane = lax.broadcasted_iota(jnp.int32, (1, BRANCH), 1)

    zc = zc_ref[0]
    u = zc[:, 0:BRANCH]
    vraw = zc[:, BRANCH:]
    xc = vraw - jnp.mean(vraw, axis=-1, keepdims=True)
    var = jnp.mean(xc * xc, axis=-1, keepdims=True)
    vn = xc * lax.rsqrt(var + SGU_EPS) * lng_ref[...] + lnb_ref[...]
    if len(out_refs) > 1:
        out_refs[1][0] = vn
    vnb = vn.astype(BF16)
    zero = jnp.zeros((), BF16)
    parts = []
    for n in range(tm // cs):
        vch = vnb[n * cs:(n + 1) * cs]
        stacked = jnp.concatenate([jnp.where(lane // HD == h, vch, zero) for h in range(HEADS)], axis=0)
        parts.append(jnp.dot(ws_ref[...], stacked, preferred_element_type=F32) + sb_ref[...])
    yc = u * (parts[0] if len(parts) == 1 else jnp.concatenate(parts, axis=0))

    zd = zd_ref[0]
    halo = halo_ref[0]
    if halo_is_prev_tile:
        halo = jnp.where(ti == 0, 0.0, halo)
    ext = jnp.concatenate([halo, zd], axis=0)
    sums = []
    acc = ext
    for sh in (1, 2, 4, 8):
        acc = acc + pltpu.roll(acc, sh, 0)
        sums.append(acc[HALO:])
    t_idx = lax.broadcasted_iota(jnp.int32, (tm, BRANCH), 0) + (ti * tm + n_hist + 1)
    lane2 = lax.broadcasted_iota(jnp.int32, (tm, BRANCH), 1)
    win = jnp.where(lane2 < HD, POOL_WINDOWS[0],
                    jnp.where(lane2 < 2 * HD, POOL_WINDOWS[1],
                              jnp.where(lane2 < 3 * HD, POOL_WINDOWS[2], POOL_WINDOWS[3])))
    cnt = jnp.minimum(t_idx, win).astype(F32)
    ssum = jnp.where(lane2 < HD, sums[0],
                     jnp.where(lane2 < 2 * HD, sums[1], jnp.where(lane2 < 3 * HD, sums[2], sums[3])))
    d = ssum / cnt - zd
    yd = jnp.dot(d.astype(BF16), pw_ref[...], preferred_element_type=F32) * psc_ref[...]

    g = zg_ref[0]
    gate = g / (1.0 + jnp.exp(-g))
    y = jnp.concatenate([ya_ref[0], yb_ref[0], yc, yd], axis=-1) * gate
    out_ref[0] = x_ref[0] + jnp.dot(y.astype(BF16), wout_ref[...], preferred_element_type=F32)


def _combine(x, ya, yb, zc, zd, halo, zg, lp, *, n_hist, want_vn):
    b, t, _ = x.shape
    tm = min(512, t)
    cs = min(SGU_CHUNK, t)
    halo_is_prev_tile = halo is None
    if halo_is_prev_tile:
        halo = zd
        per = tm // HALO
        halo_spec = pl.BlockSpec((1, HALO, BRANCH), lambda i, j: (i, jnp.maximum(j * per - 1, 0), 0))
    else:
        halo_spec = pl.BlockSpec((1, HALO, BRANCH), lambda i, j: (i, 0, 0))
    blk = lambda w: pl.BlockSpec((1, tm, w), lambda i, j: (i, j, 0))
    row = lambda w: _full((1, w))
    out_specs = [blk(D_MODEL)]
    out_shape = [jax.ShapeDtypeStruct((b, t, D_MODEL), F32)]
    if want_vn:
        out_specs.append(blk(BRANCH))
        out_shape.append(jax.ShapeDtypeStruct((b, t, BRANCH), F32))
    res = pl.pallas_call(
        functools.partial(_combine_kernel, tm=tm, cs=cs, n_hist=n_hist, halo_is_prev_tile=halo_is_prev_tile),
        grid=(b, t // tm),
        in_specs=[blk(D_MODEL), blk(BRANCH), blk(BRANCH), blk(2 * BRANCH), blk(BRANCH), halo_spec, blk(D_MODEL),
                  row(BRANCH), row(BRANCH), _full((cs, HEADS * cs)), _full((cs, BRANCH)),
                  _full((BRANCH, BRANCH)), row(BRANCH), _full((D_MODEL, D_MODEL))],
        out_specs=out_specs,
        out_shape=out_shape,
        compiler_params=_params(("parallel", "parallel")),
        name="combine",
    )(x, ya, yb, zc, zd, halo, zg, lp["sgu_ln_g"], lp["sgu_ln_b"], lp["sgu_w_cat"][cs], lp["sgu_b_tab"][cs],
      lp["pool_w_bd"], lp["pool_scale"], lp["w_out"])
    return res


def _pad_cols(w, total):
    return jnp.pad(w, ((0, 0), (0, total - w.shape[1])))


def _head_slots(w, real):
    rows = w.shape[0]
    w = w.reshape(rows, HEADS, real)
    return jnp.pad(w, ((0, 0), (0, 0), (0, HEAD_PAD - real))).reshape(rows, HEADS * HEAD_PAD)


def _packed_params(*stacked, sgu_sizes):
    packed = jax.vmap(functools.partial(_pack_layer, sgu_sizes=sgu_sizes))(*stacked)
    return [jax.tree.map(lambda v: v[l], packed) for l in range(stacked[0].shape[0])]


def _pack_layer(norm_g, w_in, w_out, rw_mu, rw_w0, rw_w2, rw_a0, rw_a2, rw_kk, rw_ka, rw_rk, rw_gn_g,
                rw_gn_b, mla_qa_g, mla_w_uq, mla_kva_g, mla_w_uk, mla_w_uv, mla_q_norm_g, mla_k_norm_g, sgu_w,
                sgu_b, sgu_ln_g, sgu_ln_b, pool_w, pool_scale, *, sgu_sizes):
    wi = w_in
    off_b = RW_SHIFT
    off_c = off_b + Q_RANK + KV_RANK + ROPE
    off_d = off_c + 2 * BRANCH
    off_g = off_d + BRANCH
    w_in_r = jnp.concatenate([
        wi[:, :off_b],
        _pad_cols(wi[:, off_b:off_b + Q_RANK], Q_PAD),
        wi[:, off_b + Q_RANK:off_b + Q_RANK + KV_RANK],
        _pad_cols(wi[:, off_b + Q_RANK + KV_RANK:off_c], LANE),
        wi[:, off_c:off_d], wi[:, off_d:off_g], wi[:, off_g:]], axis=1).astype(BF16)
    zeros = jnp.zeros((RW_LORA, BRANCH), F32)
    rw_lora = jnp.concatenate([jnp.concatenate([rw_w2, zeros], 1), jnp.concatenate([zeros, rw_a2], 1)], 0)
    row = lambda v: v.reshape(1, -1).astype(F32)
    qn = mla_q_norm_g
    kn = mla_k_norm_g
    tri = jnp.tril(jnp.ones((SGU_CHUNK, SGU_CHUNK), bool))
    ws = jnp.where(tri[None], sgu_w, 0.0)
    sgu_w_cat = {cs: jnp.concatenate([ws[h, :cs, :cs] for h in range(HEADS)], axis=1).astype(BF16)
                 for cs in sgu_sizes}
    sgu_b_tab = {cs: jnp.repeat(jnp.transpose(sgu_b)[:cs], HD, axis=1) for cs in sgu_sizes}
    eye = jnp.eye(HEADS, dtype=F32)
    pool_w_bd = jnp.einsum("gcd,gh->gchd", pool_w, eye).reshape(BRANCH, BRANCH).astype(BF16)
    return {
        "norm_g": row(norm_g), "w_in": w_in_r, "w_out": w_out.astype(BF16),
        "rw_mu": row(rw_mu), "rw_lora": rw_lora.astype(BF16), "rw_w0": row(rw_w0), "rw_a0": row(rw_a0),
        "rw_kk": row(rw_kk), "rw_ka": row(rw_ka), "rw_rk": row(rw_rk), "rw_gn_g": row(rw_gn_g),
        "rw_gn_b": row(rw_gn_b),
        "mla_qa_g": _pad_cols(row(mla_qa_g), Q_PAD),
        "mla_w_uq": jnp.pad(_head_slots(mla_w_uq, QK_HD), ((0, Q_PAD - Q_RANK), (0, 0))).astype(BF16),
        "mla_qn_g": _pad_cols(row(qn), HEAD_PAD),
        "mla_kva_g": row(mla_kva_g),
        "mla_kr_g": _pad_cols(row(kn[NOPE:]), LANE),
        "mla_w_uk": _head_slots(mla_w_uk, NOPE).astype(BF16),
        "mla_kn_g": _pad_cols(row(kn[:NOPE]), HEAD_PAD),
        "mla_w_uv": _head_slots(mla_w_uv, HD).astype(BF16),
        "mla_w_uv_t": jnp.transpose(mla_w_uv).astype(BF16),
        "sgu_ln_g": row(sgu_ln_g), "sgu_ln_b": row(sgu_ln_b), "sgu_w_cat": sgu_w_cat, "sgu_b_tab": sgu_b_tab,
        "pool_w_bd": pool_w_bd, "pool_scale": row(pool_scale),
    }


def _rope_tables(pos):
    half = ROPE // 2
    inv = ROPE_THETA ** (-jnp.arange(half, dtype=F32) / half)
    ang = pos.astype(F32)[:, None] * inv[None, :]
    cos, sin = jnp.cos(ang), jnp.sin(ang)
    t = pos.shape[0]

    def place(first, second, off, fill):
        base = jnp.full((t, LANE), fill, F32)
        base = lax.dynamic_update_slice(base, first, (0, off))
        return lax.dynamic_update_slice(base, second, (0, off + half))

    z = jnp.zeros_like(sin)
    tabs = []
    for off, fill in ((NOPE, 1.0), (0, 0.0)):
        tabs += [place(cos, cos, off, fill), place(-sin, z, off, 0.0), place(z, sin, off, 0.0)]
    return tabs


def _state_to_blockdiag(s):
    b = s.shape[0]
    eye = jnp.eye(HEADS, dtype=s.dtype)
    return jnp.einsum("bhij,hg->bhjgi", s, eye).reshape(b, BRANCH, BRANCH)


def _blockdiag_to_state(z):
    b = z.shape[0]
    z5 = z.reshape(b, HEADS, HD, HEADS, HD)
    return jnp.stack([jnp.swapaxes(z5[:, h, :, h, :], 1, 2) for h in range(HEADS)], axis=1)


def _layer(x, lp, tabs, hist):
    b, t, _ = x.shape
    za, zb, zc, zd, zg = _inproj(x.reshape(b * t, D_MODEL), lp["norm_g"], lp["w_in"])
    za = za.reshape(b, t, RW_SHIFT)
    zb = zb.reshape(b, t, 512)
    zc = zc.reshape(b, t, 2 * BRANCH)
    zd = zd.reshape(b, t, BRANCH)
    zg = zg.reshape(b, t, D_MODEL)
    if hist is None:
        shift_prev = jnp.zeros((b, 1, RW_SHIFT), F32)
        zbd0 = jnp.zeros((b, BRANCH, BRANCH), F32)
    else:
        c_ckv, c_kr, s_wkv, s_shift, s_pool = hist
        shift_prev = s_shift[:, None, :]
        zbd0 = _state_to_blockdiag(s_wkv)
    ya, zfin = _rwkv(za, shift_prev, zbd0, lp)
    q, ckv, krp, kr32 = _mla_q(zb, tabs, lp)
    if hist is None:
        k, vt = _mla_kv(ckv, krp, lp, transposed=True)
        yb = _flash_t(q, k, vt)
        halo, n_hist = None, 0
        pool_src = zd
    else:
        past = c_ckv.shape[1]
        n_all = past + t
        n_pad = -(-n_all // LANE) * LANE
        ckv_all = jnp.pad(jnp.concatenate([c_ckv, ckv], axis=1), ((0, 0), (0, n_pad - n_all), (0, 0)))
        kr_all = jnp.concatenate([jnp.pad(c_kr, ((0, 0), (0, 0), (0, LANE - ROPE))), krp], axis=1)
        kr_all = jnp.pad(kr_all, ((0, 0), (0, n_pad - n_all), (0, 0)))
        k, v = _mla_kv(ckv_all, kr_all, lp, transposed=False)
        yb = _attend_cache(q, k, v, kv_valid=n_all)
        halo = jnp.pad(s_pool, ((0, 0), (HALO - POOL_HIST, 0), (0, 0)))
        n_hist = POOL_HIST
        pool_src = jnp.concatenate([s_pool, zd], axis=1)
    res = _combine(x, ya, yb, zc, zd, halo, zg, lp, n_hist=n_hist, want_vn=hist is not None)
    state = (ckv, kr32, _blockdiag_to_state(zfin), za[:, -1], pool_src[:, pool_src.shape[1] - POOL_HIST:])
    if hist is None:
        return res[0], state
    return res[0], state + (res[1],)


def kernel(x_prompt, x_sample, cache_ckv, cache_krope, state_wkv, state_shift, state_pool, norm_g, w_in, w_out, rw_mu, rw_w0, rw_w2, rw_a0, rw_a2, rw_kk, rw_ka, rw_rk, rw_gn_g, rw_gn_b, mla_qa_g, mla_w_uq, mla_kva_g, mla_w_uk, mla_w_uv, mla_q_norm_g, mla_k_norm_g, sgu_w, sgu_b, sgu_ln_g, sgu_ln_b, pool_w, pool_scale):
    t_p = x_prompt.shape[1]
    t_s = x_sample.shape[1]
    past = cache_ckv.shape[2]
    sgu_sizes = sorted({min(SGU_CHUNK, t_p), min(SGU_CHUNK, t_s)})
    tabs_p = _rope_tables(jnp.arange(t_p))
    tabs_s = _rope_tables(past + jnp.arange(t_s))
    yp, ys = x_prompt, x_sample
    new_p = [[] for _ in range(5)]
    new_s = [[] for _ in range(6)]
    params = _packed_params(norm_g, w_in, w_out, rw_mu, rw_w0, rw_w2, rw_a0, rw_a2, rw_kk, rw_ka, rw_rk, rw_gn_g,
                            rw_gn_b, mla_qa_g, mla_w_uq, mla_kva_g, mla_w_uk, mla_w_uv, mla_q_norm_g, mla_k_norm_g,
                            sgu_w, sgu_b, sgu_ln_g, sgu_ln_b, pool_w, pool_scale, sgu_sizes=sgu_sizes)
    for l, lp in enumerate(params):
        yp, st_p = _layer(yp, lp, tabs_p, None)
        ys, st_s = _layer(ys, lp, tabs_s, (cache_ckv[l], cache_krope[l], state_wkv[l], state_shift[l], state_pool[l]))
        for lst, arr in zip(new_p, st_p):
            lst.append(arr)
        for lst, arr in zip(new_s, st_s):
            lst.append(arr)
    outs_p = [jnp.stack(a, axis=0) for a in new_p]
    outs_s = [jnp.stack(a, axis=0) for a in new_s]
    return (yp, ys, *outs_p, *outs_s)
```
